```python
import math
import jax, jax.numpy as jnp
from jax import lax
import numpy as np

D_MODEL = 1024
BATCH = 8
SEQ = 4096
DEPTH = 2

ATTN_WIDTH = D_MODEL // 2
ATTN_HEAD_DIM = 64
ATTN_QK_DIM = ATTN_HEAD_DIM // 2
ATTN_HEADS = ATTN_WIDTH // ATTN_HEAD_DIM
POOL_WIDTH = D_MODEL // 4
POOL_WINDOWS = (2, 4, 8, 16)
POOL_GROUPS = len(POOL_WINDOWS)
POOL_GROUP_DIM = POOL_WIDTH // POOL_GROUPS
CONV_WIDTH = D_MODEL // 4
CONV_GROUPS = 4
CONV_K = 3
MIX_WIDTH = ATTN_WIDTH + POOL_WIDTH + CONV_WIDTH
IN_WIDTH = 3 * ATTN_WIDTH + POOL_WIDTH + 3 * CONV_WIDTH
D_FF = ((8 * D_MODEL // 3 + 255) // 256) * 256
Q_BLOCK = 128
EPS = 1e-6

kernel_name = "hymba_style_diffattn_pool_shortconv_encoder"


def rmsnorm(x, g):
    xf = x.astype(jnp.float32)
    y = xf * lax.rsqrt(jnp.mean(xf * xf, axis=-1, keepdims=True) + EPS)
    return (y * g.astype(jnp.float32)).astype(x.dtype)


def alibi_slopes(n_heads):
    return jnp.asarray(np.array([2.0 ** (-8.0 * (h + 1) / n_heads) for h in range(n_heads)], dtype=np.float32))


def diff_attention(q, k, v, lam, slopes):
    b_, s_ = q.shape[0], q.shape[1]
    nblk = s_ // Q_BLOCK
    scale = ATTN_QK_DIM ** -0.5
    qb = q.reshape(b_, nblk, Q_BLOCK, ATTN_HEADS, 2, ATTN_QK_DIM).transpose(1, 0, 2, 3, 4, 5)
    kpos = jnp.arange(s_)

    def one_block(args):
        qi, blk = args
        qpos = blk * Q_BLOCK + jnp.arange(Q_BLOCK)
        dist = jnp.abs(qpos[:, None] - kpos[None, :]).astype(jnp.float32)
        bias = -slopes[:, None, None] * dist[None]
        s = jnp.einsum('bqhpd,bkhpd->bphqk', qi, k).astype(jnp.float32) * scale + bias
        p = jax.nn.softmax(s, axis=-1)
        w = p[:, 0] - lam * p[:, 1]
        return jnp.einsum('bhqk,bkhd->bqhd', w.astype(v.dtype), v)

    out = lax.map(one_block, (qb, jnp.arange(nblk)))
    return out.transpose(1, 0, 2, 3, 4).reshape(b_, s_, ATTN_HEADS, ATTN_HEAD_DIM)


def multiscale_pool(u, w_pool, pool_scale):
    b_, s_, _ = u.shape
    uf = u.astype(jnp.float32)
    csum = jnp.concatenate([jnp.zeros((b_, 1, POOL_WIDTH), jnp.float32), jnp.cumsum(uf, axis=1)], axis=1)
    t = jnp.arange(s_)
    pooled = []
    for g, w in enumerate(POOL_WINDOWS):
        cg = csum[..., g * POOL_GROUP_DIM:(g + 1) * POOL_GROUP_DIM]
        lo = jnp.clip(t - w // 2, 0, s_)
        hi = jnp.clip(t + w - w // 2, 0, s_)
        total = jnp.take(cg, hi, axis=1) - jnp.take(cg, lo, axis=1)
        cnt = (hi - lo).astype(jnp.float32)
        pooled.append(total / cnt[None, :, None])
    pooled = jnp.stack(pooled, axis=2)
    diff = pooled - uf.reshape(b_, s_, POOL_GROUPS, POOL_GROUP_DIM)
    mixed = jnp.einsum('bsgc,gcd->bsgd', diff, w_pool.astype(jnp.float32))
    return (mixed.reshape(b_, s_, POOL_WIDTH) * pool_scale.astype(jnp.float32)).astype(u.dtype)


def short_gated_conv(b_gate, c_gate, xh, conv_w):
    u = c_gate * xh
    up = jnp.pad(u, ((0, 0), (1, 1), (0, 0)))
    y = conv_w[0] * up[:, :-2] + conv_w[1] * up[:, 1:-1] + conv_w[2] * up[:, 2:]
    return b_gate * y


def modulate(h, shift, scale):
    return h * (1.0 + scale[:, None, :]) + shift[:, None, :]


def setup_inputs(seed: int = 0) -> dict:
    key = jax.random.key(seed)
    ks = jax.random.split(key, 20)
    f32 = jnp.float32
    n = lambda k, shape, s: jax.random.normal(k, shape, f32) * s
    return {
        "x": n(ks[0], (BATCH, SEQ, D_MODEL), 1.0),
        "c": n(ks[1], (BATCH, D_MODEL), 1.0),
        "w_ada": n(ks[2], (DEPTH, D_MODEL, 6 * D_MODEL), D_MODEL ** -0.5),
        "b_ada": n(ks[3], (DEPTH, 6 * D_MODEL), 0.02),
        "g_mix": 1.0 + n(ks[4], (DEPTH, D_MODEL), 0.1),
        "w_in": n(ks[5], (DEPTH, D_MODEL, IN_WIDTH), D_MODEL ** -0.5),
        "lambda_q1": n(ks[6], (DEPTH, ATTN_QK_DIM), 0.1),
        "lambda_k1": n(ks[7], (DEPTH, ATTN_QK_DIM), 0.1),
        "lambda_q2": n(ks[8], (DEPTH, ATTN_QK_DIM), 0.1),
        "lambda_k2": n(ks[9], (DEPTH, ATTN_QK_DIM), 0.1),
        "g_subln": 1.0 + n(ks[10], (DEPTH, ATTN_WIDTH), 0.1),
        "w_pool": n(ks[11], (DEPTH, POOL_GROUPS, POOL_GROUP_DIM, POOL_GROUP_DIM), POOL_GROUP_DIM ** -0.5),
        "pool_scale": 1.0 + n(ks[12], (DEPTH, POOL_WIDTH), 0.1),
        "conv_w": n(ks[13], (DEPTH, CONV_K, CONV_WIDTH), 0.5),
        "w_out": n(ks[14], (DEPTH, MIX_WIDTH, D_MODEL), MIX_WIDTH ** -0.5),
        "g_ffn": 1.0 + n(ks[15], (DEPTH, D_MODEL), 0.1),
        "w_gate_up": n(ks[16], (DEPTH, D_MODEL, 2 * D_FF), D_MODEL ** -0.5),
        "w_down": n(ks[17], (DEPTH, D_FF, D_MODEL), D_FF ** -0.5),
        "g_final": 1.0 + n(ks[18], (D_MODEL,), 0.1),
    }


def reference(x, c, w_ada, b_ada, g_mix, w_in, lambda_q1, lambda_k1, lambda_q2, lambda_k2,
              g_subln, w_pool, pool_scale, conv_w, w_out, g_ffn, w_gate_up, w_down, g_final):
    b_, s_, _ = x.shape
    slopes = alibi_slopes(ATTN_HEADS)
    c_act = jax.nn.silu(c)
    o_q, o_k, o_v = ATTN_WIDTH, 2 * ATTN_WIDTH, 3 * ATTN_WIDTH
    o_p = o_v + POOL_WIDTH
    o_b, o_c = o_p + CONV_WIDTH, o_p + 2 * CONV_WIDTH
    for l in range(DEPTH):
        mod = c_act @ w_ada[l] + b_ada[l]
        sh1, sc1, gt1, sh2, sc2, gt2 = jnp.split(mod, 6, axis=-1)

        h = modulate(rmsnorm(x, g_mix[l]), sh1, sc1)
        proj = h @ w_in[l]
        q = proj[..., :o_q].reshape(b_, s_, ATTN_HEADS, 2, ATTN_QK_DIM)
        k = proj[..., o_q:o_k].reshape(b_, s_, ATTN_HEADS, 2, ATTN_QK_DIM)
        v = proj[..., o_k:o_v].reshape(b_, s_, ATTN_HEADS, ATTN_HEAD_DIM)
        u_pool = proj[..., o_v:o_p]
        b_gate, c_gate, x_conv = proj[..., o_p:o_b], proj[..., o_b:o_c], proj[..., o_c:]

        lambda_init = 0.8 - 0.6 * math.exp(-0.3 * l)
        lam = (jnp.exp(jnp.sum(lambda_q1[l].astype(jnp.float32) * lambda_k1[l].astype(jnp.float32)))
               - jnp.exp(jnp.sum(lambda_q2[l].astype(jnp.float32) * lambda_k2[l].astype(jnp.float32)))
               + lambda_init)
        attn = diff_attention(q, k, v, lam, slopes)
        attn = rmsnorm(attn, g_subln[l].reshape(ATTN_HEADS, ATTN_HEAD_DIM)) * (1.0 - lambda_init)
        y_a = attn.reshape(b_, s_, ATTN_WIDTH)
        y_b = multiscale_pool(u_pool, w_pool[l], pool_scale[l])
        y_c = short_gated_conv(b_gate, c_gate, x_conv, conv_w[l])

        mixed = jnp.concatenate([y_a, y_b, y_c], axis=-1) @ w_out[l]
        x = x + gt1[:, None, :] * mixed

        h = modulate(rmsnorm(x, g_ffn[l]), sh2, sc2)
        gate, up = jnp.split(h @ w_gate_up[l], 2, axis=-1)
        x = x + gt2[:, None, :] * ((jax.nn.silu(gate) * up) @ w_down[l])

    return rmsnorm(x, g_final)
```

```python
import functools
import math

import numpy as np
import jax
import jax.numpy as jnp
from jax import lax
from jax.experimental import pallas as pl
from jax.experimental.pallas import tpu as pltpu

F32 = jnp.float32
BF16 = jnp.bfloat16

D_MODEL = 1024
DEPTH = 2
N_HEADS = 8
HEAD_DIM = 64
QK_DIM = 32
ATTN_WIDTH = N_HEADS * HEAD_DIM
POOL_WIDTH = 256
POOL_WINDOWS = (2, 4, 8, 16)
POOL_GROUP_DIM = 64
CONV_WIDTH = 256
IN_WIDTH = 3 * ATTN_WIDTH + POOL_WIDTH + 3 * CONV_WIDTH
D_FF = 2816
EPS = 1e-6
QK_SCALE = QK_DIM ** -0.5

LANES = 128
ATT_TILE = 256
V_ROWS = 80
HALO = 16
PROJ_TS = 512
VMEM_LIMIT = 56 * 1024 * 1024

BIAS_LANE_A = 64
BIAS_LANE_B = 65


def _slopes():
    return np.array([2.0 ** (-8.0 * (h + 1) / N_HEADS) for h in range(N_HEADS)], dtype=np.float64)


def _mod_kernel(c_ref, w_ref, b_ref, o_ref):
    c = c_ref[...]
    ca = c * jax.nn.sigmoid(c)
    o_ref[0] = jnp.dot(ca, w_ref[0], preferred_element_type=F32) + b_ref[0]


def _modulation(c, w_ada, b_ada):
    b_, d = c.shape
    n6 = w_ada.shape[-1] // d
    return pl.pallas_call(
        _mod_kernel,
        grid=(DEPTH, n6),
        in_specs=[
            pl.BlockSpec((b_, d), lambda l, n: (0, 0)),
            pl.BlockSpec((1, d, d), lambda l, n: (l, 0, n)),
            pl.BlockSpec((1, 1, d), lambda l, n: (l, 0, n)),
        ],
        out_specs=pl.BlockSpec((1, b_, d), lambda l, n: (l, 0, n)),
        out_shape=jax.ShapeDtypeStruct((DEPTH, b_, n6 * d), F32),
        compiler_params=pltpu.CompilerParams(
            dimension_semantics=("arbitrary", "arbitrary"), vmem_limit_bytes=VMEM_LIMIT),
        name="adaln_mod",
    )(c, w_ada, b_ada.reshape(DEPTH, 1, -1))


def _proj_kernel(x_ref, mod_ref, g_ref, w_ref, qc_ref, kc_ref, vc_ref,
                 qf_ref, kf_ref, vt_ref, pc_ref):
    ts = x_ref.shape[1]
    x = x_ref[0]
    ms = jnp.mean(x * x, axis=-1, keepdims=True)
    xn = x * lax.rsqrt(ms + EPS) * g_ref[...]
    h = (xn * (1.0 + mod_ref[0, 1:2, :]) + mod_ref[0, 0:1, :]).astype(BF16)
    proj = jnp.dot(h, w_ref[...], preferred_element_type=F32)

    data_lane = lax.broadcasted_iota(jnp.int32, (ts, LANES), 1) < HEAD_DIM
    n_kt = ts // ATT_TILE
    for c in range(N_HEADS // 2):
        qc = proj[:, c * LANES:(c + 1) * LANES] * QK_SCALE
        kc = proj[:, ATTN_WIDTH + c * LANES:ATTN_WIDTH + (c + 1) * LANES]
        vc = proj[:, 2 * ATTN_WIDTH + c * LANES:2 * ATTN_WIDTH + (c + 1) * LANES]
        pairs = ((qc, kc), (pltpu.roll(qc, HEAD_DIM, 1), pltpu.roll(kc, HEAD_DIM, 1)))
        for e, (qq, kk) in enumerate(pairs):
            hd = 2 * c + e
            qf_ref[0, hd] = jnp.where(data_lane, qq, qc_ref[hd]).astype(BF16)
            kf_ref[0, hd] = jnp.where(data_lane, kk, kc_ref[hd]).astype(BF16)
        vt = vc.T
        for e in range(2):
            hd = 2 * c + e
            for t in range(n_kt):
                vt_ref[0, hd, t, 0:HEAD_DIM, :] = (
                    vt[e * HEAD_DIM:(e + 1) * HEAD_DIM, t * ATT_TILE:(t + 1) * ATT_TILE].astype(BF16))
                vt_ref[0, hd, t, HEAD_DIM:V_ROWS, :] = vc_ref[...]

    o_p = 3 * ATTN_WIDTH
    pc_ref[0, :, 0:256] = proj[:, o_p:o_p + 256]
    pc_ref[0, :, 256:512] = proj[:, o_p + 256:o_p + 512]
    pc_ref[0, :, 512:768] = proj[:, o_p + 512:o_p + 768] * proj[:, o_p + 768:o_p + 1024]


def _projection(x, mod_l, g_mix_l, w_in_bf, q_const, k_const, v_const):
    b_, s_, d = x.shape
    ts = PROJ_TS
    n_t = s_ // ts
    n_kt = ts // ATT_TILE
    return pl.pallas_call(
        _proj_kernel,
        grid=(b_, n_t),
        in_specs=[
            pl.BlockSpec((1, ts, d), lambda b, i: (b, i, 0)),
            pl.BlockSpec((1, 6, d), lambda b, i: (b, 0, 0)),
            pl.BlockSpec((1, d), lambda b, i: (0, 0)),
            pl.BlockSpec((d, IN_WIDTH), lambda b, i: (0, 0)),
            pl.BlockSpec((N_HEADS, ts, LANES), lambda b, i: (0, 0, 0)),
            pl.BlockSpec((N_HEADS, ts, LANES), lambda b, i: (0, 0, 0)),
            pl.BlockSpec((V_ROWS - HEAD_DIM, ATT_TILE), lambda b, i: (0, 0)),
        ],
        out_specs=[
            pl.BlockSpec((1, N_HEADS, ts, LANES), lambda b, i: (b, 0, i, 0)),
            pl.BlockSpec((1, N_HEADS, ts, LANES), lambda b, i: (b, 0, i, 0)),
            pl.BlockSpec((1, N_HEADS, n_kt, V_ROWS, ATT_TILE), lambda b, i: (b, 0, i, 0, 0)),
            pl.BlockSpec((1, ts, 768), lambda b, i: (b, i, 0)),
        ],
        out_shape=[
            jax.ShapeDtypeStruct((b_, N_HEADS, s_, LANES), BF16),
            jax.ShapeDtypeStruct((b_, N_HEADS, s_, LANES), BF16),
            jax.ShapeDtypeStruct((b_, N_HEADS, s_ // ATT_TILE, V_ROWS, ATT_TILE), BF16),
            jax.ShapeDtypeStruct((b_, s_, 768), F32),
        ],
        compiler_params=pltpu.CompilerParams(
            dimension_semantics=("parallel", "parallel"), vmem_limit_bytes=VMEM_LIMIT),
        name="in_proj",
    )(x, mod_l, g_mix_l, w_in_bf, q_const, k_const, v_const)


def _attn_kernel(lam_ref, qf_ref, kf_ref, vt_ref, dtab_ref, g_ref, ct_ref, o_ref,
                 acc_ref, m_ref, *, lambda_init):
    t = ATT_TILE
    n_k = kf_ref.shape[2] // t
    qi = pl.program_id(2)

    qf = qf_ref[0, 0]
    lane = lax.broadcasted_iota(jnp.int32, qf.shape, 1)
    zero = jnp.zeros_like(qf)
    in1 = lane < QK_DIM
    in2 = jnp.logical_and(lane >= QK_DIM, lane < HEAD_DIM)
    q_diag = (jnp.where(in1, qf, zero), jnp.where(in2, qf, zero))
    q_left = (jnp.where(in2, zero, qf), jnp.where(in1, zero, qf))
    nqf = -qf
    q_right = (jnp.where(in1, qf, jnp.where(in2, zero, nqf)),
               jnp.where(in2, qf, jnp.where(in1, zero, nqf)))
    slope_t = ct_ref[0, :, 0:1]

    def tile_step(j, q_pair, c_vec, first=False, diag=False):
        kt = kf_ref[0, 0, pl.ds(pl.multiple_of(j * t, t), t), :]
        vt = vt_ref[0, 0, j]
        for br in range(2):
            s_t = lax.dot_general(kt, q_pair[br], (((1,), (1,)), ((), ())),
                                  preferred_element_type=F32)
            if diag:
                s_t = s_t + dtab_ref[0]
            m_loc = jnp.max(s_t, axis=0, keepdims=True) + c_vec
            if first:
                m_new = m_loc
            else:
                m_old = m_ref[br]
                m_new = jnp.maximum(m_old, m_loc)
                alpha = jnp.exp(m_old - m_new)
            p = jnp.exp(s_t - (m_new - c_vec)).astype(BF16)
            pv = jnp.dot(vt, p, preferred_element_type=F32)
            if first:
                acc_ref[br] = pv
            else:
                acc_ref[br] = alpha * acc_ref[br] + pv
            m_ref[br] = m_new

    tile_step(qi, q_diag, jnp.zeros((1, 1), F32), first=True, diag=True)

    def left_body(j, carry):
        tile_step(j, q_left, -slope_t * (qi - j).astype(F32))
        return carry

    lax.fori_loop(0, qi, left_body, 0)

    def right_body(j, carry):
        tile_step(j, q_right, -slope_t * (j - qi).astype(F32))
        return carry

    lax.fori_loop(qi + 1, n_k, right_body, 0)

    lv = lam_ref[...]
    lam = (jnp.exp(jnp.sum(lv[0:1] * lv[1:2], axis=1, keepdims=True))
           - jnp.exp(jnp.sum(lv[2:3] * lv[3:4], axis=1, keepdims=True)) + lambda_init)
    a1 = acc_ref[0]
    a2 = acc_ref[1]
    o = a1[0:HEAD_DIM] / a1[HEAD_DIM:HEAD_DIM + 1] - lam * (a2[0:HEAD_DIM] / a2[HEAD_DIM:HEAD_DIM + 1])
    ms = jnp.mean(o * o, axis=0, keepdims=True)
    o_ref[0] = (o * lax.rsqrt(ms + EPS) * g_ref[0]).astype(BF16)


def _attention(lam_vecs, qf, kf, vt, dtab, g_sub, ctile, lambda_init):
    b_, n_h, s_, _ = qf.shape
    t = ATT_TILE
    n_q = s_ // t
    return pl.pallas_call(
        functools.partial(_attn_kernel, lambda_init=lambda_init),
        grid=(b_, n_h, n_q),
        in_specs=[
            pl.BlockSpec((4, QK_DIM), lambda b, h, i: (0, 0)),
            pl.BlockSpec((1, 1, t, LANES), lambda b, h, i: (b, h, i, 0)),
            pl.BlockSpec((1, 1, s_, LANES), lambda b, h, i: (b, h, 0, 0)),
            pl.BlockSpec((1, 1, s_ // t, V_ROWS, t), lambda b, h, i: (b, h, 0, 0, 0)),
            pl.BlockSpec((1, t, t), lambda b, h, i: (h, 0, 0)),
            pl.BlockSpec((1, HEAD_DIM, t), lambda b, h, i: (h, 0, 0)),
            pl.BlockSpec((1, 1, LANES), lambda b, h, i: (h, 0, 0)),
        ],
        out_specs=pl.BlockSpec((1, HEAD_DIM, t), lambda b, h, i: (b, h, i)),
        out_shape=jax.ShapeDtypeStruct((b_, n_h * HEAD_DIM, s_), BF16),
        scratch_shapes=[pltpu.VMEM((2, V_ROWS, t), F32), pltpu.VMEM((2, 1, t), F32)],
        compiler_params=pltpu.CompilerParams(
            dimension_semantics=("parallel", "parallel", "arbitrary"), vmem_limit_bytes=VMEM_LIMIT),
        name="diff_attn",
    )(lam_vecs, qf, kf, vt, dtab, g_sub, ctile)


def _mix_kernel(yat_ref, pc_ref, pprev_ref, pnext_ref, x_ref, mod_ref, wo_ref, wp_ref, ps_ref, cw_ref,
                o_ref, *, seq_len):
    ts = x_ref.shape[1]
    i = pl.program_id(1)
    n_i = pl.num_programs(1)
    has_prev = (i > 0).astype(F32)
    has_next = (i < n_i - 1).astype(F32)

    pc = pc_ref[0]
    prev = pprev_ref[0] * has_prev
    nxt = pnext_ref[0] * has_next
    n_ext = ts + 2 * HALO
    u_ext = jnp.concatenate([prev[:, 0:256], pc[:, 0:256], nxt[:, 0:256]], axis=0)
    cu_ext = jnp.concatenate([prev[:, 512:768], pc[:, 512:768], nxt[:, 512:768]], axis=0)

    a2 = u_ext + pltpu.roll(u_ext, 1, 0)
    a4 = a2 + pltpu.roll(a2, 2, 0)
    a8 = a4 + pltpu.roll(a4, 4, 0)
    a16 = a8 + pltpu.roll(a8, 8, 0)
    lane = lax.broadcasted_iota(jnp.int32, (ts, POOL_WIDTH), 1)
    grp = lane // POOL_GROUP_DIM
    win = []
    for w, a in zip(POOL_WINDOWS, (a2, a4, a8, a16)):
        lead = w // 2 - 1
        r = a if lead == 0 else pltpu.roll(a, n_ext - lead, 0)
        win.append(r[HALO:HALO + ts])
    total = jnp.where(grp == 0, win[0], jnp.where(grp == 1, win[1], jnp.where(grp == 2, win[2], win[3])))
    half = jnp.where(grp == 0, 1, jnp.where(grp == 1, 2, jnp.where(grp == 2, 4, 8)))
    pos = i * ts + lax.broadcasted_iota(jnp.int32, (ts, POOL_WIDTH), 0)
    cnt = (jnp.minimum(pos + half, seq_len) - jnp.maximum(pos - half, 0)).astype(F32)
    diff = total / cnt - pc[:, 0:256]
    y_b = jnp.dot(diff.astype(BF16), wp_ref[...], preferred_element_type=F32) * ps_ref[...]

    cw = cw_ref[...]
    conv = (cw[0:1] * pltpu.roll(cu_ext, 1, 0) + cw[1:2] * cu_ext
            + cw[2:3] * pltpu.roll(cu_ext, n_ext - 1, 0))[HALO:HALO + ts]
    y_c = pc[:, 256:512] * conv

    mixed = lax.dot_general(yat_ref[0], wo_ref[0:ATTN_WIDTH, :], (((0,), (0,)), ((), ())),
                            preferred_element_type=F32)
    mixed = mixed + jnp.dot(y_b.astype(BF16), wo_ref[ATTN_WIDTH:ATTN_WIDTH + POOL_WIDTH, :],
                            preferred_element_type=F32)
    mixed = mixed + jnp.dot(y_c.astype(BF16), wo_ref[ATTN_WIDTH + POOL_WIDTH:, :],
                            preferred_element_type=F32)
    o_ref[0] = x_ref[0] + mod_ref[0, 2:3, :] * mixed


def _mix(yat, pc, x, mod_l, w_out_bf, w_pool_bd, pool_scale_l, conv_w_l):
    b_, s_, d = x.shape
    ts = PROJ_TS
    n_t = s_ // ts
    hb = ts // HALO
    n_hb = s_ // HALO
    return pl.pallas_call(
        functools.partial(_mix_kernel, seq_len=s_),
        grid=(b_, n_t),
        in_specs=[
            pl.BlockSpec((1, ATTN_WIDTH, ts), lambda b, i: (b, 0, i)),
            pl.BlockSpec((1, ts, 768), lambda b, i: (b, i, 0)),
            pl.BlockSpec((1, HALO, 768), lambda b, i: (b, jnp.maximum(i * hb - 1, 0), 0)),
            pl.BlockSpec((1, HALO, 768), lambda b, i: (b, jnp.minimum((i + 1) * hb, n_hb - 1), 0)),
            pl.BlockSpec((1, ts, d), lambda b, i: (b, i, 0)),
            pl.BlockSpec((1, 6, d), lambda b, i: (b, 0, 0)),
            pl.BlockSpec((d, d), lambda b, i: (0, 0)),
            pl.BlockSpec((POOL_WIDTH, POOL_WIDTH), lambda b, i: (0, 0)),
            pl.BlockSpec((1, POOL_WIDTH), lambda b, i: (0, 0)),
            pl.BlockSpec((3, CONV_WIDTH), lambda b, i: (0, 0)),
        ],
        out_specs=pl.BlockSpec((1, ts, d), lambda b, i: (b, i, 0)),
        out_shape=jax.ShapeDtypeStruct((b_, s_, d), F32),
        compiler_params=pltpu.CompilerParams(
            dimension_semantics=("parallel", "parallel"), vmem_limit_bytes=VMEM_LIMIT),
        name="mix_out_proj",
    )(yat, pc, pc, pc, x, mod_l, w_out_bf, w_pool_bd, pool_scale_l, conv_w_l)


def _ffn_kernel(x_ref, mod_ref, g_ref, wgu_ref, wd_ref, gf_ref, o_ref, *, final_norm):
    x = x_ref[0]
    ms = jnp.mean(x * x, axis=-1, keepdims=True)
    xn = x * lax.rsqrt(ms + EPS) * g_ref[...]
    h = (xn * (1.0 + mod_ref[0, 4:5, :]) + mod_ref[0, 3:4, :]).astype(BF16)
    gate = jnp.dot(h, wgu_ref[:, 0:D_FF], preferred_element_type=F32)
    up = jnp.dot(h, wgu_ref[:, D_FF:2 * D_FF], preferred_element_type=F32)
    act = (gate * jax.nn.sigmoid(gate) * up).astype(BF16)
    y = x + mod_ref[0, 5:6, :] * jnp.dot(act, wd_ref[...], preferred_element_type=F32)
    if final_norm:
        ms2 = jnp.mean(y * y, axis=-1, keepdims=True)
        y = y * lax.rsqrt(ms2 + EPS) * gf_ref[...]
    o_ref[0] = y


def _ffn(x, mod_l, g_ffn_l, w_gu_bf, w_down_bf, g_final, final_norm):
    b_, s_, d = x.shape
    ts = 256
    n_t = s_ // ts
    return pl.pallas_call(
        functools.partial(_ffn_kernel, final_norm=final_norm),
        grid=(b_, n_t),
        in_specs=[
            pl.BlockSpec((1, ts, d), lambda b, i: (b, i, 0)),
            pl.BlockSpec((1, 6, d), lambda b, i: (b, 0, 0)),
            pl.BlockSpec((1, d), lambda b, i: (0, 0)),
            pl.BlockSpec((d, 2 * D_FF), lambda b, i: (0, 0)),
            pl.BlockSpec((D_FF, d), lambda b, i: (0, 0)),
            pl.BlockSpec((1, d), lambda b, i: (0, 0)),
        ],
        out_specs=pl.BlockSpec((1, ts, d), lambda b, i: (b, i, 0)),
        out_shape=jax.ShapeDtypeStruct((b_, s_, d), F32),
        compiler_params=pltpu.CompilerParams(
            dimension_semantics=("parallel", "parallel"), vmem_limit_bytes=VMEM_LIMIT),
        name="swiglu_ffn",
    )(x, mod_l, g_ffn_l, w_gu_bf, w_down_bf, g_final)


def _attention_constants(ts):
    slopes = _slopes()
    t = ATT_TILE
    r = np.arange(ts) % t
    q_const = np.zeros((N_HEADS, ts, LANES), np.float32)
    k_const = np.zeros((N_HEADS, ts, LANES), np.float32)
    for h in range(N_HEADS):
        q_const[h, :, BIAS_LANE_A] = -slopes[h] * r
        q_const[h, :, BIAS_LANE_B] = 1.0
        k_const[h, :, BIAS_LANE_A] = 1.0
        k_const[h, :, BIAS_LANE_B] = slopes[h] * r
    v_const = np.zeros((V_ROWS - HEAD_DIM, t), np.float32)
    v_const[0, :] = 1.0
    idx = np.arange(t)
    dist = np.abs(idx[:, None] - idx[None, :]).astype(np.float64)
    dtab = (-slopes[:, None, None] * dist[None]).astype(np.float32)
    ctile = np.broadcast_to((slopes * t).astype(np.float32)[:, None, None], (N_HEADS, 1, LANES))
    return (jnp.asarray(q_const), jnp.asarray(k_const), jnp.asarray(v_const, dtype=BF16),
            jnp.asarray(dtab), jnp.asarray(np.ascontiguousarray(ctile)))


def kernel(x, c, w_ada, b_ada, g_mix, w_in, lambda_q1, lambda_k1, lambda_q2, lambda_k2, g_subln, w_pool,
           pool_scale, conv_w, w_out, g_ffn, w_gate_up, w_down, g_final):
    b_, s_, d = x.shape
    q_const, k_const, v_const, dtab, ctile = _attention_constants(PROJ_TS)
    mod = _modulation(c, w_ada, b_ada).reshape(DEPTH, b_, 6, d)
    g_final2 = g_final.reshape(1, d)
    for l in range(DEPTH):
        lambda_init = 0.8 - 0.6 * math.exp(-0.3 * l)
        lam_vecs = jnp.stack([lambda_q1[l], lambda_k1[l], lambda_q2[l], lambda_k2[l]]).astype(F32)
        g_sub = jnp.broadcast_to(
            (g_subln[l].astype(F32) * (1.0 - lambda_init)).reshape(N_HEADS, HEAD_DIM, 1),
            (N_HEADS, HEAD_DIM, ATT_TILE))
        w_pool_bd = jax.scipy.linalg.block_diag(*[w_pool[l, g] for g in range(len(POOL_WINDOWS))]).astype(BF16)

        qf, kf, vt, pc = _projection(x, mod[l], g_mix[l].reshape(1, d), w_in[l].astype(BF16),
                                     q_const, k_const, v_const)
        yat = _attention(lam_vecs, qf, kf, vt, dtab, g_sub, ctile, lambda_init)
        x = _mix(yat, pc, x, mod[l], w_out[l].astype(BF16), w_pool_bd,
                 pool_scale[l].reshape(1, POOL_WIDTH), conv_w[l])
        x = _ffn(x, mod[l], g_ffn[l].reshape(1, d), w_gate_up[l].astype(BF16), w_down[l].astype(BF16),
                 g_final2, final_norm=(l == DEPTH - 1))
    return x
```

```python
import functools
import math

import ml_dtypes
import numpy as np
import jax
import jax.numpy as jnp
from jax import lax
from jax.experimental import pallas as pl
from jax.experimental.pallas import tpu as pltpu

F32 = jnp.float32
BF16 = jnp.bfloat16

D_MODEL = 1024
DEPTH = 2
N_HEADS = 8
HEAD_DIM = 64
QK_DIM = 32
ATTN_WIDTH = N_HEADS * HEAD_DIM
POOL_WIDTH = 256
POOL_WINDOWS = (2, 4, 8, 16)
POOL_GROUP_DIM = 64
CONV_WIDTH = 256
IN_WIDTH = 3 * ATTN_WIDTH + POOL_WIDTH + 3 * CONV_WIDTH
D_FF = 2816
EPS = 1e-6
LOG2E = math.log2(math.e)
QK_SCALE = QK_DIM ** -0.5

LANES = 128
ATT_TILE = 512
V_ROWS = 80
HALO = 16
PROJ_TS = 512
VMEM_LIMIT = 56 * 1024 * 1024
NEG_BIG = -1e30

BIAS_LANE0 = HEAD_DIM
N_PIECES = 3


def _slopes():
    return np.array([2.0 ** (-8.0 * (h + 1) / N_HEADS) for h in range(N_HEADS)], dtype=np.float64)


def _bf16_pieces(v):
    rest = v.astype(np.float32)
    out = []
    for _ in range(N_PIECES):
        p = rest.astype(ml_dtypes.bfloat16).astype(np.float32)
        out.append(p)
        rest = rest - p
    return out


def _mod_kernel(c_ref, w_ref, b_ref, o_ref):
    c = c_ref[...]
    ca = c * jax.nn.sigmoid(c)
    o_ref[0] = jnp.dot(ca, w_ref[0], preferred_element_type=F32) + b_ref[0]


def _modulation(c, w_ada, b_ada):
    b_, d = c.shape
    n6 = w_ada.shape[-1] // d
    return pl.pallas_call(
        _mod_kernel,
        grid=(DEPTH, n6),
        in_specs=[
            pl.BlockSpec((b_, d), lambda l, n: (0, 0)),
            pl.BlockSpec((1, d, d), lambda l, n: (l, 0, n)),
            pl.BlockSpec((1, 1, d), lambda l, n: (l, 0, n)),
        ],
        out_specs=pl.BlockSpec((1, b_, d), lambda l, n: (l, 0, n)),
        out_shape=jax.ShapeDtypeStruct((DEPTH, b_, n6 * d), F32),
        compiler_params=pltpu.CompilerParams(
            dimension_semantics=("arbitrary", "arbitrary"), vmem_limit_bytes=VMEM_LIMIT),
        name="adaln_mod",
    )(c, w_ada, b_ada.reshape(DEPTH, 1, -1))


def _proj_kernel(x_ref, mod_ref, g_ref, w_ref, qc_ref, kc_ref, vc_ref,
                 qf_ref, kf_ref, vt_ref, pc_ref):
    ts = x_ref.shape[1]
    x = x_ref[0]
    ms = jnp.mean(x * x, axis=-1, keepdims=True)
    xn = x * lax.rsqrt(ms + EPS) * g_ref[...]
    h = (xn * (1.0 + mod_ref[0, 1:2, :]) + mod_ref[0, 0:1, :]).astype(BF16)
    proj = jnp.dot(h, w_ref[...], preferred_element_type=F32)

    data_lane = lax.broadcasted_iota(jnp.int32, (ts, LANES), 1) < HEAD_DIM
    n_kt = ts // ATT_TILE
    for c in range(N_HEADS // 2):
        qc = proj[:, c * LANES:(c + 1) * LANES] * (QK_SCALE * LOG2E)
        kc = proj[:, ATTN_WIDTH + c * LANES:ATTN_WIDTH + (c + 1) * LANES]
        vc = proj[:, 2 * ATTN_WIDTH + c * LANES:2 * ATTN_WIDTH + (c + 1) * LANES]
        pairs = ((qc, kc), (pltpu.roll(qc, HEAD_DIM, 1), pltpu.roll(kc, HEAD_DIM, 1)))
        for e, (qq, kk) in enumerate(pairs):
            hd = 2 * c + e
            qf_ref[0, hd] = jnp.where(data_lane, qq, qc_ref[hd]).astype(BF16)
            kf_ref[0, hd] = jnp.where(data_lane, kk, kc_ref[hd]).astype(BF16)
        vt = vc.T
        for e in range(2):
            hd = 2 * c + e
            for t in range(n_kt):
                vt_ref[0, hd, t, 0:HEAD_DIM, :] = (
                    vt[e * HEAD_DIM:(e + 1) * HEAD_DIM, t * ATT_TILE:(t + 1) * ATT_TILE].astype(BF16))
                vt_ref[0, hd, t, HEAD_DIM:V_ROWS, :] = vc_ref[...]

    o_p = 3 * ATTN_WIDTH
    pc_ref[0, :, 0:256] = proj[:, o_p:o_p + 256]
    pc_ref[0, :, 256:512] = proj[:, o_p + 256:o_p + 512]
    pc_ref[0, :, 512:768] = proj[:, o_p + 512:o_p + 768] * proj[:, o_p + 768:o_p + 1024]


def _projection(x, mod_l, g_mix_l, w_in_bf, q_const, k_const, v_const):
    b_, s_, d = x.shape
    ts = PROJ_TS
    n_t = s_ // ts
    n_kt = ts // ATT_TILE
    return pl.pallas_call(
        _proj_kernel,
        grid=(b_, n_t),
        in_specs=[
            pl.BlockSpec((1, ts, d), lambda b, i: (b, i, 0)),
            pl.BlockSpec((1, 6, d), lambda b, i: (b, 0, 0)),
            pl.BlockSpec((1, d), lambda b, i: (0, 0)),
            pl.BlockSpec((d, IN_WIDTH), lambda b, i: (0, 0)),
            pl.BlockSpec((N_HEADS, ts, LANES), lambda b, i: (0, 0, 0)),
            pl.BlockSpec((N_HEADS, ts, LANES), lambda b, i: (0, 0, 0)),
            pl.BlockSpec((V_ROWS - HEAD_DIM, ATT_TILE), lambda b, i: (0, 0)),
        ],
        out_specs=[
            pl.BlockSpec((1, N_HEADS, ts, LANES), lambda b, i: (b, 0, i, 0)),
            pl.BlockSpec((1, N_HEADS, ts, LANES), lambda b, i: (b, 0, i, 0)),
            pl.BlockSpec((1, N_HEADS, n_kt, V_ROWS, ATT_TILE), lambda b, i: (b, 0, i, 0, 0)),
            pl.BlockSpec((1, ts, 768), lambda b, i: (b, i, 0)),
        ],
        out_shape=[
            jax.ShapeDtypeStruct((b_, N_HEADS, s_, LANES), BF16),
            jax.ShapeDtypeStruct((b_, N_HEADS, s_, LANES), BF16),
            jax.ShapeDtypeStruct((b_, N_HEADS, s_ // ATT_TILE, V_ROWS, ATT_TILE), BF16),
            jax.ShapeDtypeStruct((b_, s_, 768), F32),
        ],
        compiler_params=pltpu.CompilerParams(
            dimension_semantics=("parallel", "parallel"), vmem_limit_bytes=VMEM_LIMIT),
        name="in_proj",
    )(x, mod_l, g_mix_l, w_in_bf, q_const, k_const, v_const)


def _attn_kernel(lam_ref, qf_ref, kf_ref, vt_ref, dtab_ref, g_ref, ct_ref, o_ref,
                 kvar_ref, s0_ref, s1_ref, ml_ref, acc_ref, m_ref, *, lambda_init):
    t = ATT_TILE
    n_t = kf_ref.shape[2] // t
    s_bufs = (s0_ref, s1_ref)

    kf = kf_ref[0, 0]
    k_lane = lax.broadcasted_iota(jnp.int32, kf.shape, 1)
    kvar_ref[0] = kf
    kvar_ref[1] = jnp.where(k_lane >= BIAS_LANE0, -kf, kf)
    kvar_ref[2] = jnp.where(k_lane >= BIAS_LANE0, jnp.zeros_like(kf), kf)

    slope_t = ct_ref[0, :, 0:1]
    lv = lam_ref[...]
    lam = (jnp.exp(jnp.sum(lv[0:1] * lv[1:2], axis=1, keepdims=True))
           - jnp.exp(jnp.sum(lv[2:3] * lv[3:4], axis=1, keepdims=True)) + lambda_init)

    def branch_queries(qi):
        qf = qf_ref[0, 0, pl.ds(pl.multiple_of(qi * t, t), t), :]
        lane = lax.broadcasted_iota(jnp.int32, qf.shape, 1)
        zero = jnp.zeros_like(qf)
        in1 = lane < QK_DIM
        in2 = jnp.logical_and(lane >= QK_DIM, lane < HEAD_DIM)
        return jnp.where(in2, zero, qf), jnp.where(in1, zero, qf)

    def stage_a(q_pair, kj, variant, slot, diag):
        kt = kvar_ref[variant, pl.ds(pl.multiple_of(kj * t, t), t), :]
        for br in range(2):
            s_t = lax.dot_general(kt, q_pair[br], (((1,), (1,)), ((), ())),
                                  preferred_element_type=F32)
            if diag:
                s_t = s_t + dtab_ref[0]
            s_bufs[slot][br] = s_t
            ml_ref[slot, br] = jnp.max(s_t, axis=0, keepdims=True)

    def stage_bc(kj, dist, slot):
        c_vec = -slope_t * dist.astype(F32)
        vt = vt_ref[0, 0, kj]
        for br in range(2):
            m_old = m_ref[br]
            m_new = jnp.maximum(m_old, ml_ref[slot, br] + c_vec)
            alpha = jnp.exp2(m_old - m_new)
            p = jnp.exp2(s_bufs[slot][br] - (m_new - c_vec)).astype(BF16)
            pv = jnp.dot(vt, p, preferred_element_type=F32)
            acc_ref[br] = alpha * acc_ref[br] + pv
            m_ref[br] = m_new

    def reset_state():
        m_ref[...] = jnp.full(m_ref.shape, NEG_BIG, F32)
        acc_ref[...] = jnp.zeros(acc_ref.shape, F32)

    def finalize(qi):
        a1 = acc_ref[0]
        a2 = acc_ref[1]
        o = (a1[0:HEAD_DIM] / a1[HEAD_DIM:HEAD_DIM + 1]
             - lam * (a2[0:HEAD_DIM] / a2[HEAD_DIM:HEAD_DIM + 1]))
        ms = jnp.mean(o * o, axis=0, keepdims=True)
        o_ref[0, 0, qi] = (o * lax.rsqrt(ms + EPS) * g_ref[0]).astype(BF16)

    reset_state()
    stage_a(branch_queries(0), 0, 2, 0, True)

    def query_tile(qi, carry):
        q_pair = branch_queries(qi)
        prev_kj, prev_dist = qi, jnp.int32(0)
        for k in range(1, n_t):
            after = (k - 1) >= qi
            kj = jnp.where(after, k, k - 1)
            stage_a(q_pair, kj, after.astype(jnp.int32), k % 2, False)
            stage_bc(prev_kj, prev_dist, (k - 1) % 2)
            prev_kj, prev_dist = kj, jnp.abs(qi - kj)
        q_next = jnp.minimum(qi + 1, n_t - 1)
        stage_a(branch_queries(q_next), q_next, 2, 0, True)
        stage_bc(prev_kj, prev_dist, (n_t - 1) % 2)
        finalize(qi)
        reset_state()
        return carry

    lax.fori_loop(0, n_t, query_tile, 0)


def _attention(lam_vecs, qf, kf, vt, dtab, g_sub, ctile, lambda_init):
    b_, n_h, s_, _ = qf.shape
    t = ATT_TILE
    n_t = s_ // t
    assert n_t % 2 == 0
    return pl.pallas_call(
        functools.partial(_attn_kernel, lambda_init=lambda_init),
        grid=(b_, n_h),
        in_specs=[
            pl.BlockSpec((4, QK_DIM), lambda b, h: (0, 0)),
            pl.BlockSpec((1, 1, s_, LANES), lambda b, h: (b, h, 0, 0)),
            pl.BlockSpec((1, 1, s_, LANES), lambda b, h: (b, h, 0, 0)),
            pl.BlockSpec((1, 1, n_t, V_ROWS, t), lambda b, h: (b, h, 0, 0, 0)),
            pl.BlockSpec((1, t, t), lambda b, h: (h, 0, 0)),
            pl.BlockSpec((1, HEAD_DIM, t), lambda b, h: (h, 0, 0)),
            pl.BlockSpec((1, 1, LANES), lambda b, h: (h, 0, 0)),
        ],
        out_specs=pl.BlockSpec((1, 1, n_t, HEAD_DIM, t), lambda b, h: (b, h, 0, 0, 0)),
        out_shape=jax.ShapeDtypeStruct((b_, n_h, n_t, HEAD_DIM, t), BF16),
        scratch_shapes=[
            pltpu.VMEM((3, s_, LANES), BF16),
            pltpu.VMEM((2, t, t), F32),
            pltpu.VMEM((2, t, t), F32),
            pltpu.VMEM((2, 2, 1, t), F32),
            pltpu.VMEM((2, V_ROWS, t), F32),
            pltpu.VMEM((2, 1, t), F32),
        ],
        compiler_params=pltpu.CompilerParams(
            dimension_semantics=("parallel", "parallel"), vmem_limit_bytes=VMEM_LIMIT),
        name="diff_attn",
    )(lam_vecs, qf, kf, vt, dtab, g_sub, ctile)


def _mix_kernel(yat_ref, pc_ref, pprev_ref, pnext_ref, x_ref, mod_ref, wo_ref, wp_ref, ps_ref, cw_ref,
                o_ref, *, seq_len):
    ts = x_ref.shape[1]
    i = pl.program_id(1)
    n_i = pl.num_programs(1)
    has_prev = (i > 0).astype(F32)
    has_next = (i < n_i - 1).astype(F32)

    pc = pc_ref[0]
    prev = pprev_ref[0] * has_prev
    nxt = pnext_ref[0] * has_next
    n_ext = ts + 2 * HALO
    u_ext = jnp.concatenate([prev[:, 0:256], pc[:, 0:256], nxt[:, 0:256]], axis=0)
    cu_ext = jnp.concatenate([prev[:, 512:768], pc[:, 512:768], nxt[:, 512:768]], axis=0)

    a2 = u_ext + pltpu.roll(u_ext, 1, 0)
    a4 = a2 + pltpu.roll(a2, 2, 0)
    a8 = a4 + pltpu.roll(a4, 4, 0)
    a16 = a8 + pltpu.roll(a8, 8, 0)
    lane = lax.broadcasted_iota(jnp.int32, (ts, POOL_WIDTH), 1)
    grp = lane // POOL_GROUP_DIM
    win = []
    for w, a in zip(POOL_WINDOWS, (a2, a4, a8, a16)):
        lead = w // 2 - 1
        r = a if lead == 0 else pltpu.roll(a, n_ext - lead, 0)
        win.append(r[HALO:HALO + ts])
    total = jnp.where(grp == 0, win[0], jnp.where(grp == 1, win[1], jnp.where(grp == 2, win[2], win[3])))
    half = jnp.where(grp == 0, 1, jnp.where(grp == 1, 2, jnp.where(grp == 2, 4, 8)))
    pos = i * ts + lax.broadcasted_iota(jnp.int32, (ts, POOL_WIDTH), 0)
    cnt = (jnp.minimum(pos + half, seq_len) - jnp.maximum(pos - half, 0)).astype(F32)
    diff = total / cnt - pc[:, 0:256]
    y_b = jnp.dot(diff.astype(BF16), wp_ref[...], preferred_element_type=F32) * ps_ref[...]

    cw = cw_ref[...]
    conv = (cw[0:1] * pltpu.roll(cu_ext, 1, 0) + cw[1:2] * cu_ext
            + cw[2:3] * pltpu.roll(cu_ext, n_ext - 1, 0))[HALO:HALO + ts]
    y_c = pc[:, 256:512] * conv

    yat = yat_ref[0, :, 0].reshape(ATTN_WIDTH, ts)
    mixed = lax.dot_general(yat, wo_ref[0:ATTN_WIDTH, :], (((0,), (0,)), ((), ())),
                            preferred_element_type=F32)
    mixed = mixed + jnp.dot(y_b.astype(BF16), wo_ref[ATTN_WIDTH:ATTN_WIDTH + POOL_WIDTH, :],
                            preferred_element_type=F32)
    mixed = mixed + jnp.dot(y_c.astype(BF16), wo_ref[ATTN_WIDTH + POOL_WIDTH:, :],
                            preferred_element_type=F32)
    o_ref[0] = x_ref[0] + mod_ref[0, 2:3, :] * mixed


def _mix(yat, pc, x, mod_l, w_out_bf, w_pool_bd, pool_scale_l, conv_w_l):
    b_, s_, d = x.shape
    ts = PROJ_TS
    assert ts == ATT_TILE
    n_t = s_ // ts
    hb = ts // HALO
    n_hb = s_ // HALO
    return pl.pallas_call(
        functools.partial(_mix_kernel, seq_len=s_),
        grid=(b_, n_t),
        in_specs=[
            pl.BlockSpec((1, N_HEADS, 1, HEAD_DIM, ts), lambda b, i: (b, 0, i, 0, 0)),
            pl.BlockSpec((1, ts, 768), lambda b, i: (b, i, 0)),
            pl.BlockSpec((1, HALO, 768), lambda b, i: (b, jnp.maximum(i * hb - 1, 0), 0)),
            pl.BlockSpec((1, HALO, 768), lambda b, i: (b, jnp.minimum((i + 1) * hb, n_hb - 1), 0)),
            pl.BlockSpec((1, ts, d), lambda b, i: (b, i, 0)),
            pl.BlockSpec((1, 6, d), lambda b, i: (b, 0, 0)),
            pl.BlockSpec((d, d), lambda b, i: (0, 0)),
            pl.BlockSpec((POOL_WIDTH, POOL_WIDTH), lambda b, i: (0, 0)),
            pl.BlockSpec((1, POOL_WIDTH), lambda b, i: (0, 0)),
            pl.BlockSpec((3, CONV_WIDTH), lambda b, i: (0, 0)),
        ],
        out_specs=pl.BlockSpec((1, ts, d), lambda b, i: (b, i, 0)),
        out_shape=jax.ShapeDtypeStruct((b_, s_, d), F32),
        compiler_params=pltpu.CompilerParams(
            dimension_semantics=("parallel", "parallel"), vmem_limit_bytes=VMEM_LIMIT),
        name="mix_out_proj",
    )(yat, pc, pc, pc, x, mod_l, w_out_bf, w_pool_bd, pool_scale_l, conv_w_l)


def _ffn_kernel(x_ref, mod_ref, g_ref, wgu_ref, wd_ref, gf_ref, o_ref, *, final_norm):
    x = x_ref[0]
    ms = jnp.mean(x * x, axis=-1, keepdims=True)
    xn = x * lax.rsqrt(ms + EPS) * g_ref[...]
    h = (xn * (1.0 + mod_ref[0, 4:5, :]) + mod_ref[0, 3:4, :]).astype(BF16)
    gate = jnp.dot(h, wgu_ref[:, 0:D_FF], preferred_element_type=F32)
    up = jnp.dot(h, wgu_ref[:, D_FF:2 * D_FF], preferred_element_type=F32)
    act = (gate * jax.nn.sigmoid(gate) * up).astype(BF16)
    y = x + mod_ref[0, 5:6, :] * jnp.dot(act, wd_ref[...], preferred_element_type=F32)
    if final_norm:
        ms2 = jnp.mean(y * y, axis=-1, keepdims=True)
        y = y * lax.rsqrt(ms2 + EPS) * gf_ref[...]
    o_ref[0] = y


def _ffn(x, mod_l, g_ffn_l, w_gu_bf, w_down_bf, g_final, final_norm):
    b_, s_, d = x.shape
    ts = 256
    n_t = s_ // ts
    return pl.pallas_call(
        functools.partial(_ffn_kernel, final_norm=final_norm),
        grid=(b_, n_t),
        in_specs=[
            pl.BlockSpec((1, ts, d), lambda b, i: (b, i, 0)),
            pl.BlockSpec((1, 6, d), lambda b, i: (b, 0, 0)),
            pl.BlockSpec((1, d), lambda b, i: (0, 0)),
            pl.BlockSpec((d, 2 * D_FF), lambda b, i: (0, 0)),
            pl.BlockSpec((D_FF, d), lambda b, i: (0, 0)),
            pl.BlockSpec((1, d), lambda b, i: (0, 0)),
        ],
        out_specs=pl.BlockSpec((1, ts, d), lambda b, i: (b, i, 0)),
        out_shape=jax.ShapeDtypeStruct((b_, s_, d), F32),
        compiler_params=pltpu.CompilerParams(
            dimension_semantics=("parallel", "parallel"), vmem_limit_bytes=VMEM_LIMIT),
        name="swiglu_ffn",
    )(x, mod_l, g_ffn_l, w_gu_bf, w_down_bf, g_final)


def _attention_constants(ts):
    slopes = _slopes()
    t = ATT_TILE
    r = (np.arange(ts) % t).astype(np.float64)
    q_const = np.zeros((N_HEADS, ts, LANES), np.float32)
    k_const = np.zeros((N_HEADS, ts, LANES), np.float32)
    for h in range(N_HEADS):
        row_term = _bf16_pieces((-slopes[h] * LOG2E * r).astype(np.float32))
        col_term = _bf16_pieces((slopes[h] * LOG2E * r).astype(np.float32))
        for p in range(N_PIECES):
            q_const[h, :, BIAS_LANE0 + p] = row_term[p]
            k_const[h, :, BIAS_LANE0 + p] = 1.0
            q_const[h, :, BIAS_LANE0 + N_PIECES + p] = 1.0
            k_const[h, :, BIAS_LANE0 + N_PIECES + p] = col_term[p]
    v_const = np.zeros((V_ROWS - HEAD_DIM, t), np.float32)
    v_const[0, :] = 1.0
    idx = np.arange(t)
    dist = np.abs(idx[:, None] - idx[None, :]).astype(np.float64)
    dtab = (-slopes[:, None, None] * LOG2E * dist[None]).astype(np.float32)
    ctile = np.broadcast_to((slopes * t * LOG2E).astype(np.float32)[:, None, None], (N_HEADS, 1, LANES))
    return (jnp.asarray(q_const), jnp.asarray(k_const), jnp.asarray(v_const, dtype=BF16),
            jnp.asarray(dtab), jnp.asarray(np.ascontiguousarray(ctile)))


def kernel(x, c, w_ada, b_ada, g_mix, w_in, lambda_q1, lambda_k1, lambda_q2, lambda_k2, g_subln, w_pool,
           pool_scale, conv_w, w_out, g_ffn, w_gate_up, w_down, g_final):
    b_, s_, d = x.shape
    q_const, k_const, v_const, dtab, ctile = _attention_constants(PROJ_TS)
    mod = _modulation(c, w_ada, b_ada).reshape(DEPTH, b_, 6, d)
    g_final2 = g_final.reshape(1, d)
    for l in range(DEPTH):
        lambda_init = 0.8 - 0.6 * math.exp(-0.3 * l)
        lam_vecs = jnp.stack([lambda_q1[l], lambda_k1[l], lambda_q2[l], lambda_k2[l]]).astype(F32)
        g_sub = jnp.broadcast_to(
            (g_subln[l].astype(F32) * (1.0 - lambda_init)).reshape(N_HEADS, HEAD_DIM, 1),
            (N_HEADS, HEAD_DIM, ATT_TILE))
        w_pool_bd = jax.scipy.linalg.block_diag(*[w_pool[l, g] for g in range(len(POOL_WINDOWS))]).astype(BF16)

        qf, kf, vt, pc = _projection(x, mod[l], g_mix[l].reshape(1, d), w_in[l].astype(BF16),
                                     q_const, k_const, v_const)
        yat = _attention(lam_vecs, qf, kf, vt, dtab, g_sub, ctile, lambda_init)
        x = _mix(yat, pc, x, mod[l], w_out[l].astype(BF16), w_pool_bd,
                 pool_scale[l].reshape(1, POOL_WIDTH), conv_w[l])
        x = _ffn(x, mod[l], g_ffn[l].reshape(1, d), w_gate_up[l].astype(BF16), w_down[l].astype(BF16),
                 g_final2, final_norm=(l == DEPTH - 1))
    return x
```

```python
import functools
import math

import numpy as np
import jax
import jax.numpy as jnp
from jax import lax
from jax.experimental import pallas as pl
from jax.experimental.pallas import tpu as pltpu

F32 = jnp.float32
BF16 = jnp.bfloat16

D_MODEL = 1024
DEPTH = 2
N_HEADS = 8
HEAD_DIM = 64
QK_DIM = 32
ATTN_WIDTH = N_HEADS * HEAD_DIM
POOL_WIDTH = 256
POOL_WINDOWS = (2, 4, 8, 16)
POOL_GROUP_DIM = 64
CONV_WIDTH = 256
IN_WIDTH = 3 * ATTN_WIDTH + POOL_WIDTH + 3 * CONV_WIDTH
D_FF = 2816
EPS = 1e-6
LOG2E = math.log2(math.e)
QK_SCALE = QK_DIM ** -0.5

LANES = 128
Q_TILE = 512
K_TILE = 256
V_ROWS = 80
HALO = 16
PROJ_TS = 512
VMEM_LIMIT = 56 * 1024 * 1024
NEG_BIG = -1e30
L_MAX = 2.0 ** 100
F32_HUGE = 3.0e38
LOOKAHEAD = 1

N_PIECES = 3
LANE_ROW = HEAD_DIM
LANE_COL = LANE_ROW + N_PIECES
LANE_M1 = LANE_COL + N_PIECES
LANE_M2 = LANE_M1 + N_PIECES
LANE_END = LANE_M2 + N_PIECES


def _slopes():
    return np.array([2.0 ** (-8.0 * (h + 1) / N_HEADS) for h in range(N_HEADS)], dtype=np.float64)


def _mod_kernel(c_ref, w_ref, b_ref, o_ref):
    c = c_ref[...]
    ca = c * jax.nn.sigmoid(c)
    o_ref[0] = jnp.dot(ca, w_ref[0], preferred_element_type=F32) + b_ref[0]


def _modulation(c, w_ada, b_ada):
    b_, d = c.shape
    n6 = w_ada.shape[-1] // d
    return pl.pallas_call(
        _mod_kernel,
        grid=(DEPTH, n6),
        in_specs=[
            pl.BlockSpec((b_, d), lambda l, n: (0, 0)),
            pl.BlockSpec((1, d, d), lambda l, n: (l, 0, n)),
            pl.BlockSpec((1, 1, d), lambda l, n: (l, 0, n)),
        ],
        out_specs=pl.BlockSpec((1, b_, d), lambda l, n: (l, 0, n)),
        out_shape=jax.ShapeDtypeStruct((DEPTH, b_, n6 * d), F32),
        compiler_params=pltpu.CompilerParams(
            dimension_semantics=("arbitrary", "arbitrary"), vmem_limit_bytes=VMEM_LIMIT),
        name="adaln_mod",
    )(c, w_ada, b_ada.reshape(DEPTH, 1, -1))


def _round_bf16(v):
    return v.astype(BF16).astype(F32)


def _pieces(v):
    out = []
    for _ in range(N_PIECES):
        p = _round_bf16(v)
        out.append(p)
        v = v - p
    return out


def _place(lane, first_lane, pieces, rest):
    for n, p in enumerate(pieces):
        rest = jnp.where(lane == first_lane + n, p, rest)
    return rest


def _proj_kernel(x_ref, mod_ref, g_ref, w_ref, vc_ref, qf_ref, kf_ref, vt_ref, pc_ref, *, seq_len):
    ts = x_ref.shape[1]
    x = x_ref[0]
    ms = jnp.mean(x * x, axis=-1, keepdims=True)
    xn = x * lax.rsqrt(ms + EPS) * g_ref[...]
    h = (xn * (1.0 + mod_ref[0, 1:2, :]) + mod_ref[0, 0:1, :]).astype(BF16)
    proj = jnp.dot(h, w_ref[...], preferred_element_type=F32)

    lane = lax.broadcasted_iota(jnp.int32, (ts, LANES), 1)
    row = lax.broadcasted_iota(jnp.int32, (ts, LANES), 0)
    pos = (pl.program_id(1) * ts + row - seq_len // 2).astype(F32) * LOG2E
    pos_pieces = _pieces(pos)
    zeros = jnp.zeros((ts, LANES), F32)
    q_pos = _place(lane, LANE_ROW, [-p for p in pos_pieces], zeros)
    k_pos = _place(lane, LANE_COL, pos_pieces, zeros)
    q_ones = jnp.where(jnp.logical_and(lane >= LANE_COL, lane < LANE_M1), 1.0, 0.0)
    k_ones = jnp.where(jnp.logical_or(jnp.logical_and(lane >= LANE_ROW, lane < LANE_COL),
                                      jnp.logical_and(lane >= LANE_M1, lane < LANE_END)), 1.0, 0.0)
    in1 = lane < QK_DIM
    in2 = jnp.logical_and(lane >= QK_DIM, lane < HEAD_DIM)
    data_lane = lane < HEAD_DIM

    n_kt = ts // K_TILE
    for c in range(N_HEADS // 2):
        qc = proj[:, c * LANES:(c + 1) * LANES] * (QK_SCALE * LOG2E)
        kc = proj[:, ATTN_WIDTH + c * LANES:ATTN_WIDTH + (c + 1) * LANES]
        vc = proj[:, 2 * ATTN_WIDTH + c * LANES:2 * ATTN_WIDTH + (c + 1) * LANES]
        pairs = ((qc, kc), (pltpu.roll(qc, HEAD_DIM, 1), pltpu.roll(kc, HEAD_DIM, 1)))
        for e, (qq, kk) in enumerate(pairs):
            hd = 2 * c + e
            slope = float(_slopes()[hd])
            qq = _round_bf16(qq)
            kk = _round_bf16(kk)
            own = qq * kk
            own1 = jnp.sum(jnp.where(in1, own, 0.0), axis=1, keepdims=True)
            own2 = jnp.sum(jnp.where(in2, own, 0.0), axis=1, keepdims=True)
            q_bias = _place(lane, LANE_M1, _pieces(-own1),
                            _place(lane, LANE_M2, _pieces(-own2), q_pos * slope + q_ones))
            qf_ref[0, hd] = jnp.where(data_lane, qq, q_bias).astype(BF16)
            kf_ref[0, hd] = jnp.where(data_lane, kk, k_pos * slope + k_ones).astype(BF16)
        vt = vc.T
        for e in range(2):
            hd = 2 * c + e
            for t in range(n_kt):
                vt_ref[0, hd, t, 0:HEAD_DIM, :] = (
                    vt[e * HEAD_DIM:(e + 1) * HEAD_DIM, t * K_TILE:(t + 1) * K_TILE].astype(BF16))
                vt_ref[0, hd, t, HEAD_DIM:V_ROWS, :] = vc_ref[...]

    o_p = 3 * ATTN_WIDTH
    pc_ref[0, :, 0:256] = proj[:, o_p:o_p + 256]
    pc_ref[0, :, 256:512] = proj[:, o_p + 256:o_p + 512]
    pc_ref[0, :, 512:768] = proj[:, o_p + 512:o_p + 768] * proj[:, o_p + 768:o_p + 1024]


def _projection(x, mod_l, g_mix_l, w_in_bf, v_const):
    b_, s_, d = x.shape
    ts = PROJ_TS
    n_t = s_ // ts
    n_kt = ts // K_TILE
    return pl.pallas_call(
        functools.partial(_proj_kernel, seq_len=s_),
        grid=(b_, n_t),
        in_specs=[
            pl.BlockSpec((1, ts, d), lambda b, i: (b, i, 0)),
            pl.BlockSpec((1, 6, d), lambda b, i: (b, 0, 0)),
            pl.BlockSpec((1, d), lambda b, i: (0, 0)),
            pl.BlockSpec((d, IN_WIDTH), lambda b, i: (0, 0)),
            pl.BlockSpec((V_ROWS - HEAD_DIM, K_TILE), lambda b, i: (0, 0)),
        ],
        out_specs=[
            pl.BlockSpec((1, N_HEADS, ts, LANES), lambda b, i: (b, 0, i, 0)),
            pl.BlockSpec((1, N_HEADS, ts, LANES), lambda b, i: (b, 0, i, 0)),
            pl.BlockSpec((1, N_HEADS, n_kt, V_ROWS, K_TILE), lambda b, i: (b, 0, i, 0, 0)),
            pl.BlockSpec((1, ts, 768), lambda b, i: (b, i, 0)),
        ],
        out_shape=[
            jax.ShapeDtypeStruct((b_, N_HEADS, s_, LANES), BF16),
            jax.ShapeDtypeStruct((b_, N_HEADS, s_, LANES), BF16),
            jax.ShapeDtypeStruct((b_, N_HEADS, s_ // K_TILE, V_ROWS, K_TILE), BF16),
            jax.ShapeDtypeStruct((b_, s_, 768), F32),
        ],
        compiler_params=pltpu.CompilerParams(
            dimension_semantics=("parallel", "parallel"), vmem_limit_bytes=VMEM_LIMIT),
        name="in_proj",
    )(x, mod_l, g_mix_l, w_in_bf, v_const)


def _attn_kernel(lam_ref, qf_ref, kf_ref, vt_ref, dtab_ref, g_ref, o_ref,
                 kvar_ref, qvar_ref, flag_ref, acc_ref, m_ref, *, lambda_init):
    tq, tk = Q_TILE, K_TILE
    n_q = qf_ref.shape[2] // tq
    n_k = kf_ref.shape[2] // tk
    n_diag = tq // tk

    kf = kf_ref[0, 0]
    lane = lax.broadcasted_iota(jnp.int32, kf.shape, 1)
    pos_lane = jnp.logical_and(lane >= LANE_ROW, lane < LANE_M1)
    kvar_ref[0] = kf
    kvar_ref[1] = jnp.where(pos_lane, -kf, kf)
    kvar_ref[2] = jnp.where(pos_lane, jnp.zeros_like(kf), kf)
    qf = qf_ref[0, 0]
    other2 = jnp.logical_or(jnp.logical_and(lane >= QK_DIM, lane < HEAD_DIM),
                            jnp.logical_and(lane >= LANE_M2, lane < LANE_END))
    other1 = jnp.logical_or(lane < QK_DIM, jnp.logical_and(lane >= LANE_M1, lane < LANE_M2))
    qvar_ref[0] = jnp.where(other2, jnp.zeros_like(qf), qf)
    qvar_ref[1] = jnp.where(other1, jnp.zeros_like(qf), qf)
    flag_ref[...] = jnp.zeros(flag_ref.shape, F32)

    lv = lam_ref[...]
    lam = (jnp.exp(jnp.sum(lv[0:1] * lv[1:2], axis=1, keepdims=True))
           - jnp.exp(jnp.sum(lv[2:3] * lv[3:4], axis=1, keepdims=True)) + lambda_init)

    def key_tile(qi, k):
        if isinstance(k, int) and k < n_diag:
            return n_diag * qi + k, 2, k
        idx = k - n_diag
        after = jnp.where(idx >= n_diag * qi, 1, 0)
        return idx + n_diag * after, after, None

    def shifted_scores(qi, k, br):
        kj, variant, table = key_tile(qi, k)
        kt = kvar_ref[variant, pl.ds(pl.multiple_of(kj * tk, tk), tk), :]
        q_br = qvar_ref[br, pl.ds(pl.multiple_of(qi * tq, tq), tq), :]
        x = lax.dot_general(kt, q_br, (((1,), (1,)), ((), ())), preferred_element_type=F32)
        if table is not None:
            x = x + dtab_ref[0, table]
        return x, kj

    def write_out(qi, a1, a2):
        o = (a1[0:HEAD_DIM] / a1[HEAD_DIM:HEAD_DIM + 1]
             - lam * (a2[0:HEAD_DIM] / a2[HEAD_DIM:HEAD_DIM + 1]))
        ms = jnp.mean(o * o, axis=0, keepdims=True)
        o_ref[0, 0, qi] = (o * lax.rsqrt(ms + EPS) * g_ref[0]).astype(BF16)

    def fast_query_tile(qi, carry):
        acc = [None, None]
        ahead = [[shifted_scores(qi, k, br) for br in range(2)] for k in range(LOOKAHEAD)]
        for k in range(n_k):
            if k + LOOKAHEAD < n_k:
                ahead.append([shifted_scores(qi, k + LOOKAHEAD, br) for br in range(2)])
            for br, (x, kj) in enumerate(ahead.pop(0)):
                pv = jnp.dot(vt_ref[0, 0, kj], jnp.exp2(x).astype(BF16), preferred_element_type=F32)
                acc[br] = pv if acc[br] is None else acc[br] + pv
        bad = jnp.zeros((1, tq), F32)
        for a in acc:
            den = a[HEAD_DIM:HEAD_DIM + 1]
            mag = jnp.sum(jnp.abs(a), axis=0, keepdims=True)
            ok = jnp.logical_and(jnp.logical_and(den > 1.0 / L_MAX, den < L_MAX), mag < F32_HUGE)
            bad = jnp.maximum(bad, jnp.where(ok, 0.0, 1.0))
        flag_ref[...] = jnp.maximum(flag_ref[...], bad)
        write_out(qi, acc[0], acc[1])
        return carry

    lax.fori_loop(0, n_q, fast_query_tile, 0)

    @pl.when(jnp.max(flag_ref[...]) > 0.0)
    def _safe_path():
        def tile_step(qi, k):
            for br in range(2):
                x, kj = shifted_scores(qi, k, br)
                m_old = m_ref[br]
                m_new = jnp.maximum(m_old, jnp.max(x, axis=0, keepdims=True))
                p = jnp.exp2(x - m_new).astype(BF16)
                pv = jnp.dot(vt_ref[0, 0, kj], p, preferred_element_type=F32)
                acc_ref[br] = jnp.exp2(m_old - m_new) * acc_ref[br] + pv
                m_ref[br] = m_new

        def safe_query_tile(qi, carry):
            m_ref[...] = jnp.full(m_ref.shape, NEG_BIG, F32)
            acc_ref[...] = jnp.zeros(acc_ref.shape, F32)
            for k in range(n_diag):
                tile_step(qi, k)

            def off_diagonal(k, c):
                tile_step(qi, k)
                return c

            lax.fori_loop(n_diag, n_k, off_diagonal, 0)
            write_out(qi, acc_ref[0], acc_ref[1])
            return carry

        lax.fori_loop(0, n_q, safe_query_tile, 0)


def _attention(lam_vecs, qf, kf, vt, dtab, g_sub, lambda_init):
    b_, n_h, s_, _ = qf.shape
    tq, tk = Q_TILE, K_TILE
    n_q, n_k = s_ // tq, s_ // tk
    assert tq % tk == 0
    return pl.pallas_call(
        functools.partial(_attn_kernel, lambda_init=lambda_init),
        grid=(b_, n_h),
        in_specs=[
            pl.BlockSpec((4, QK_DIM), lambda b, h: (0, 0)),
            pl.BlockSpec((1, 1, s_, LANES), lambda b, h: (b, h, 0, 0)),
            pl.BlockSpec((1, 1, s_, LANES), lambda b, h: (b, h, 0, 0)),
            pl.BlockSpec((1, 1, n_k, V_ROWS, tk), lambda b, h: (b, h, 0, 0, 0)),
            pl.BlockSpec((1, tq // tk, tk, tq), lambda b, h: (h, 0, 0, 0)),
            pl.BlockSpec((1, HEAD_DIM, tq), lambda b, h: (h, 0, 0)),
        ],
        out_specs=pl.BlockSpec((1, 1, n_q, HEAD_DIM, tq), lambda b, h: (b, h, 0, 0, 0)),
        out_shape=jax.ShapeDtypeStruct((b_, n_h, n_q, HEAD_DIM, tq), BF16),
        scratch_shapes=[
            pltpu.VMEM((3, s_, LANES), BF16),
            pltpu.VMEM((2, s_, LANES), BF16),
            pltpu.VMEM((1, tq), F32),
            pltpu.VMEM((2, V_ROWS, tq), F32),
            pltpu.VMEM((2, 1, tq), F32),
        ],
        compiler_params=pltpu.CompilerParams(
            dimension_semantics=("parallel", "parallel"), vmem_limit_bytes=VMEM_LIMIT),
        name="diff_attn",
    )(lam_vecs, qf, kf, vt, dtab, g_sub)


def _mix_kernel(yat_ref, pc_ref, pprev_ref, pnext_ref, x_ref, mod_ref, wo_ref, wp_ref, ps_ref, cw_ref,
                o_ref, *, seq_len):
    ts = x_ref.shape[1]
    i = pl.program_id(1)
    n_i = pl.num_programs(1)
    has_prev = (i > 0).astype(F32)
    has_next = (i < n_i - 1).astype(F32)

    pc = pc_ref[0]
    prev = pprev_ref[0] * has_prev
    nxt = pnext_ref[0] * has_next
    n_ext = ts + 2 * HALO
    u_ext = jnp.concatenate([prev[:, 0:256], pc[:, 0:256], nxt[:, 0:256]], axis=0)
    cu_ext = jnp.concatenate([prev[:, 512:768], pc[:, 512:768], nxt[:, 512:768]], axis=0)

    a2 = u_ext + pltpu.roll(u_ext, 1, 0)
    a4 = a2 + pltpu.roll(a2, 2, 0)
    a8 = a4 + pltpu.roll(a4, 4, 0)
    a16 = a8 + pltpu.roll(a8, 8, 0)
    lane = lax.broadcasted_iota(jnp.int32, (ts, POOL_WIDTH), 1)
    grp = lane // POOL_GROUP_DIM
    win = []
    for w, a in zip(POOL_WINDOWS, (a2, a4, a8, a16)):
        lead = w // 2 - 1
        r = a if lead == 0 else pltpu.roll(a, n_ext - lead, 0)
        win.append(r[HALO:HALO + ts])
    total = jnp.where(grp == 0, win[0], jnp.where(grp == 1, win[1], jnp.where(grp == 2, win[2], win[3])))
    half = jnp.where(grp == 0, 1, jnp.where(grp == 1, 2, jnp.where(grp == 2, 4, 8)))
    pos = i * ts + lax.broadcasted_iota(jnp.int32, (ts, POOL_WIDTH), 0)
    cnt = (jnp.minimum(pos + half, seq_len) - jnp.maximum(pos - half, 0)).astype(F32)
    diff = total / cnt - pc[:, 0:256]
    y_b = jnp.dot(diff.astype(BF16), wp_ref[...], preferred_element_type=F32) * ps_ref[...]

    cw = cw_ref[...]
    conv = (cw[0:1] * pltpu.roll(cu_ext, 1, 0) + cw[1:2] * cu_ext
            + cw[2:3] * pltpu.roll(cu_ext, n_ext - 1, 0))[HALO:HALO + ts]
    y_c = pc[:, 256:512] * conv

    yat = yat_ref[0, :, 0].reshape(ATTN_WIDTH, ts)
    mixed = lax.dot_general(yat, wo_ref[0:ATTN_WIDTH, :], (((0,), (0,)), ((), ())),
                            preferred_element_type=F32)
    mixed = mixed + jnp.dot(y_b.astype(BF16), wo_ref[ATTN_WIDTH:ATTN_WIDTH + POOL_WIDTH, :],
                            preferred_element_type=F32)
    mixed = mixed + jnp.dot(y_c.astype(BF16), wo_ref[ATTN_WIDTH + POOL_WIDTH:, :],
                            preferred_element_type=F32)
    o_ref[0] = x_ref[0] + mod_ref[0, 2:3, :] * mixed


def _mix(yat, pc, x, mod_l, w_out_bf, w_pool_bd, pool_scale_l, conv_w_l):
    b_, s_, d = x.shape
    ts = PROJ_TS
    assert ts == Q_TILE
    n_t = s_ // ts
    hb = ts // HALO
    n_hb = s_ // HALO
    return pl.pallas_call(
        functools.partial(_mix_kernel, seq_len=s_),
        grid=(b_, n_t),
        in_specs=[
            pl.BlockSpec((1, N_HEADS, 1, HEAD_DIM, ts), lambda b, i: (b, 0, i, 0, 0)),
            pl.BlockSpec((1, ts, 768), lambda b, i: (b, i, 0)),
            pl.BlockSpec((1, HALO, 768), lambda b, i: (b, jnp.maximum(i * hb - 1, 0), 0)),
            pl.BlockSpec((1, HALO, 768), lambda b, i: (b, jnp.minimum((i + 1) * hb, n_hb - 1), 0)),
            pl.BlockSpec((1, ts, d), lambda b, i: (b, i, 0)),
            pl.BlockSpec((1, 6, d), lambda b, i: (b, 0, 0)),
            pl.BlockSpec((d, d), lambda b, i: (0, 0)),
            pl.BlockSpec((POOL_WIDTH, POOL_WIDTH), lambda b, i: (0, 0)),
            pl.BlockSpec((1, POOL_WIDTH), lambda b, i: (0, 0)),
            pl.BlockSpec((3, CONV_WIDTH), lambda b, i: (0, 0)),
        ],
        out_specs=pl.BlockSpec((1, ts, d), lambda b, i: (b, i, 0)),
        out_shape=jax.ShapeDtypeStruct((b_, s_, d), F32),
        compiler_params=pltpu.CompilerParams(
            dimension_semantics=("parallel", "parallel"), vmem_limit_bytes=VMEM_LIMIT),
        name="mix_out_proj",
    )(yat, pc, pc, pc, x, mod_l, w_out_bf, w_pool_bd, pool_scale_l, conv_w_l)


def _ffn_kernel(x_ref, mod_ref, g_ref, wgu_ref, wd_ref, gf_ref, o_ref, *, final_norm):
    x = x_ref[0]
    ms = jnp.mean(x * x, axis=-1, keepdims=True)
    xn = x * lax.rsqrt(ms + EPS) * g_ref[...]
    h = (xn * (1.0 + mod_ref[0, 4:5, :]) + mod_ref[0, 3:4, :]).astype(BF16)
    gate = jnp.dot(h, wgu_ref[:, 0:D_FF], preferred_element_type=F32)
    up = jnp.dot(h, wgu_ref[:, D_FF:2 * D_FF], preferred_element_type=F32)
    act = (gate * jax.nn.sigmoid(gate) * up).astype(BF16)
    y = x + mod_ref[0, 5:6, :] * jnp.dot(act, wd_ref[...], preferred_element_type=F32)
    if final_norm:
        ms2 = jnp.mean(y * y, axis=-1, keepdims=True)
        y = y * lax.rsqrt(ms2 + EPS) * gf_ref[...]
    o_ref[0] = y


def _ffn(x, mod_l, g_ffn_l, w_gu_bf, w_down_bf, g_final, final_norm):
    b_, s_, d = x.shape
    ts = 256
    n_t = s_ // ts
    return pl.pallas_call(
        functools.partial(_ffn_kernel, final_norm=final_norm),
        grid=(b_, n_t),
        in_specs=[
            pl.BlockSpec((1, ts, d), lambda b, i: (b, i, 0)),
            pl.BlockSpec((1, 6, d), lambda b, i: (b, 0, 0)),
            pl.BlockSpec((1, d), lambda b, i: (0, 0)),
            pl.BlockSpec((d, 2 * D_FF), lambda b, i: (0, 0)),
            pl.BlockSpec((D_FF, d), lambda b, i: (0, 0)),
            pl.BlockSpec((1, d), lambda b, i: (0, 0)),
        ],
        out_specs=pl.BlockSpec((1, ts, d), lambda b, i: (b, i, 0)),
        out_shape=jax.ShapeDtypeStruct((b_, s_, d), F32),
        compiler_params=pltpu.CompilerParams(
            dimension_semantics=("parallel", "parallel"), vmem_limit_bytes=VMEM_LIMIT),
        name="swiglu_ffn",
    )(x, mod_l, g_ffn_l, w_gu_bf, w_down_bf, g_final)


def _attention_constants():
    slopes = _slopes()
    tq, tk = Q_TILE, K_TILE
    v_const = np.zeros((V_ROWS - HEAD_DIM, tk), np.float32)
    v_const[0, :] = 1.0
    key_pos = np.arange(tq).reshape(tq // tk, tk)
    dist = np.abs(key_pos[:, :, None] - np.arange(tq)[None, None, :]).astype(np.float64)
    dtab = (-slopes[:, None, None, None] * LOG2E * dist[None]).astype(np.float32)
    return jnp.asarray(v_const, dtype=BF16), jnp.asarray(dtab)


def kernel(x, c, w_ada, b_ada, g_mix, w_in, lambda_q1, lambda_k1, lambda_q2, lambda_k2, g_subln, w_pool,
           pool_scale, conv_w, w_out, g_ffn, w_gate_up, w_down, g_final):
    b_, s_, d = x.shape
    v_const, dtab = _attention_constants()
    mod = _modulation(c, w_ada, b_ada).reshape(DEPTH, b_, 6, d)
    g_final2 = g_final.reshape(1, d)
    for l in range(DEPTH):
        lambda_init = 0.8 - 0.6 * math.exp(-0.3 * l)
        lam_vecs = jnp.stack([lambda_q1[l], lambda_k1[l], lambda_q2[l], lambda_k2[l]]).astype(F32)
        g_sub = jnp.broadcast_to(
            (g_subln[l].astype(F32) * (1.0 - lambda_init)).reshape(N_HEADS, HEAD_DIM, 1),
            (N_HEADS, HEAD_DIM, Q_TILE))
        w_pool_bd = jax.scipy.linalg.block_diag(*[w_pool[l, g] for g in range(len(POOL_WINDOWS))]).astype(BF16)

        qf, kf, vt, pc = _projection(x, mod[l], g_mix[l].reshape(1, d), w_in[l].astype(BF16), v_const)
        yat = _attention(lam_vecs, qf, kf, vt, dtab, g_sub, lambda_init)
        x = _mix(yat, pc, x, mod[l], w_out[l].astype(BF16), w_pool_bd,
                 pool_scale[l].reshape(1, POOL_WIDTH), conv_w[l])
        x = _ffn(x, mod[l], g_ffn[l].reshape(1, d), w_gate_up[l].astype(BF16), w_down[l].astype(BF16),
                 g_final2, final_norm=(l == DEPTH - 1))
    return x
```

```python
import functools
import math

import numpy as np
import jax
import jax.numpy as jnp
from jax import lax
from jax.experimental import pallas as pl
from jax.experimental.pallas import tpu as pltpu

F32 = jnp.float32
BF16 = jnp.bfloat16

D_MODEL = 1024
DEPTH = 2
N_HEADS = 8
HEAD_DIM = 64
QK_DIM = 32
ATTN_WIDTH = N_HEADS * HEAD_DIM
POOL_WIDTH = 256
POOL_WINDOWS = (2, 4, 8, 16)
POOL_GROUP_DIM = 64
CONV_WIDTH = 256
IN_WIDTH = 3 * ATTN_WIDTH + POOL_WIDTH + 3 * CONV_WIDTH
D_FF = 2816
EPS = 1e-6
LOG2E = math.log2(math.e)
QK_SCALE = QK_DIM ** -0.5

LANES = 128
Q_TILE = 512
K_TILE = 256
V_ROWS = 80
HALO = 16
PROJ_TS = 512
FFN_TS = 512
VMEM_LIMIT = 56 * 1024 * 1024
NEG_BIG = -1e30
L_MAX = 2.0 ** 100
F32_HUGE = 3.0e38
LOOKAHEAD = 1

N_PIECES = 3
LANE_ROW = HEAD_DIM
LANE_COL = LANE_ROW + N_PIECES
LANE_M1 = LANE_COL + N_PIECES
LANE_M2 = LANE_M1 + 1
LANE_END = LANE_M2 + 1


def _slopes():
    return np.array([2.0 ** (-8.0 * (h + 1) / N_HEADS) for h in range(N_HEADS)], dtype=np.float64)


def _mod_kernel(c_ref, w_ref, b_ref, o_ref):
    c = c_ref[...]
    ca = c * jax.nn.sigmoid(c)
    o_ref[0] = jnp.dot(ca, w_ref[0], preferred_element_type=F32) + b_ref[0]


def _modulation(c, w_ada, b_ada):
    b_, d = c.shape
    n6 = w_ada.shape[-1] // d
    return pl.pallas_call(
        _mod_kernel,
        grid=(DEPTH, n6),
        in_specs=[
            pl.BlockSpec((b_, d), lambda l, n: (0, 0)),
            pl.BlockSpec((1, d, d), lambda l, n: (l, 0, n)),
            pl.BlockSpec((1, 1, d), lambda l, n: (l, 0, n)),
        ],
        out_specs=pl.BlockSpec((1, b_, d), lambda l, n: (l, 0, n)),
        out_shape=jax.ShapeDtypeStruct((DEPTH, b_, n6 * d), F32),
        compiler_params=pltpu.CompilerParams(
            dimension_semantics=("arbitrary", "arbitrary"), vmem_limit_bytes=VMEM_LIMIT),
        name="adaln_mod",
    )(c, w_ada, b_ada.reshape(DEPTH, 1, -1))


def _round_bf16(v):
    return v.astype(BF16).astype(F32)


def _pieces(v):
    out = []
    for _ in range(N_PIECES):
        p = _round_bf16(v)
        out.append(p)
        v = v - p
    return out


def _place(lane, first_lane, pieces, rest):
    for n, p in enumerate(pieces):
        rest = jnp.where(lane == first_lane + n, p, rest)
    return rest


def _proj_kernel(x_ref, mod_ref, g_ref, w_ref, vc_ref, qt_ref, kf_ref, vt_ref, pc_ref, *, seq_len):
    ts = x_ref.shape[1]
    x = x_ref[0]
    ms = jnp.mean(x * x, axis=-1, keepdims=True)
    xn = x * lax.rsqrt(ms + EPS) * g_ref[...]
    h = (xn * (1.0 + mod_ref[0, 1:2, :]) + mod_ref[0, 0:1, :]).astype(BF16)
    proj = jnp.dot(h, w_ref[...], preferred_element_type=F32)

    lane = lax.broadcasted_iota(jnp.int32, (ts, LANES), 1)
    row = lax.broadcasted_iota(jnp.int32, (ts, LANES), 0)
    pos = (pl.program_id(1) * ts + row - seq_len // 2).astype(F32) * LOG2E
    pos_pieces = _pieces(pos)
    zeros = jnp.zeros((ts, LANES), F32)
    q_pos = _place(lane, LANE_ROW, [-p for p in pos_pieces], zeros)
    k_pos = _place(lane, LANE_COL, pos_pieces, zeros)
    q_ones = jnp.where(jnp.logical_and(lane >= LANE_COL, lane < LANE_M1), 1.0, 0.0)
    k_ones = jnp.where(jnp.logical_or(jnp.logical_and(lane >= LANE_ROW, lane < LANE_COL),
                                      jnp.logical_and(lane >= LANE_M1, lane < LANE_END)), 1.0, 0.0)
    in1 = lane < QK_DIM
    in2 = jnp.logical_and(lane >= QK_DIM, lane < HEAD_DIM)
    data_lane = lane < HEAD_DIM

    n_kt = ts // K_TILE
    for c in range(N_HEADS // 2):
        qc = proj[:, c * LANES:(c + 1) * LANES] * (QK_SCALE * LOG2E)
        kc = proj[:, ATTN_WIDTH + c * LANES:ATTN_WIDTH + (c + 1) * LANES]
        vc = proj[:, 2 * ATTN_WIDTH + c * LANES:2 * ATTN_WIDTH + (c + 1) * LANES]
        pairs = ((qc, kc), (pltpu.roll(qc, HEAD_DIM, 1), pltpu.roll(kc, HEAD_DIM, 1)))
        for e, (qq, kk) in enumerate(pairs):
            hd = 2 * c + e
            slope = float(_slopes()[hd])
            own = qq * kk
            own1 = jnp.sum(jnp.where(in1, own, 0.0), axis=1, keepdims=True)
            own2 = jnp.sum(jnp.where(in2, own, 0.0), axis=1, keepdims=True)
            q_bias = jnp.where(lane == LANE_M1, -own1, jnp.where(lane == LANE_M2, -own2, q_pos * slope + q_ones))
            q_full = jnp.where(data_lane, qq, q_bias)
            qt_ref[0, hd, 0] = q_full.T.astype(BF16)
            kf_ref[0, hd] = jnp.where(data_lane, kk, k_pos * slope + k_ones).astype(BF16)
        vt = vc.T
        for e in range(2):
            hd = 2 * c + e
            for t in range(n_kt):
                vt_ref[0, hd, t, 0:HEAD_DIM, :] = (
                    vt[e * HEAD_DIM:(e + 1) * HEAD_DIM, t * K_TILE:(t + 1) * K_TILE].astype(BF16))
                vt_ref[0, hd, t, HEAD_DIM:V_ROWS, :] = vc_ref[...]

    o_p = 3 * ATTN_WIDTH
    pc_ref[0, :, 0:256] = proj[:, o_p:o_p + 256]
    pc_ref[0, :, 256:512] = proj[:, o_p + 256:o_p + 512]
    pc_ref[0, :, 512:768] = proj[:, o_p + 512:o_p + 768] * proj[:, o_p + 768:o_p + 1024]


def _projection(x, mod_l, g_mix_l, w_in_bf, v_const):
    b_, s_, d = x.shape
    ts = PROJ_TS
    assert ts == Q_TILE
    n_t = s_ // ts
    n_kt = ts // K_TILE
    return pl.pallas_call(
        functools.partial(_proj_kernel, seq_len=s_),
        grid=(b_, n_t),
        in_specs=[
            pl.BlockSpec((1, ts, d), lambda b, i: (b, i, 0)),
            pl.BlockSpec((1, 6, d), lambda b, i: (b, 0, 0)),
            pl.BlockSpec((1, d), lambda b, i: (0, 0)),
            pl.BlockSpec((d, IN_WIDTH), lambda b, i: (0, 0)),
            pl.BlockSpec((V_ROWS - HEAD_DIM, K_TILE), lambda b, i: (0, 0)),
        ],
        out_specs=[
            pl.BlockSpec((1, N_HEADS, 1, LANES, ts), lambda b, i: (b, 0, i, 0, 0)),
            pl.BlockSpec((1, N_HEADS, ts, LANES), lambda b, i: (b, 0, i, 0)),
            pl.BlockSpec((1, N_HEADS, n_kt, V_ROWS, K_TILE), lambda b, i: (b, 0, i, 0, 0)),
            pl.BlockSpec((1, ts, 768), lambda b, i: (b, i, 0)),
        ],
        out_shape=[
            jax.ShapeDtypeStruct((b_, N_HEADS, n_t, LANES, ts), BF16),
            jax.ShapeDtypeStruct((b_, N_HEADS, s_, LANES), BF16),
            jax.ShapeDtypeStruct((b_, N_HEADS, s_ // K_TILE, V_ROWS, K_TILE), BF16),
            jax.ShapeDtypeStruct((b_, s_, 768), F32),
        ],
        compiler_params=pltpu.CompilerParams(
            dimension_semantics=("parallel", "parallel"), vmem_limit_bytes=VMEM_LIMIT),
        name="in_proj",
    )(x, mod_l, g_mix_l, w_in_bf, v_const)


def _attn_kernel(lam_ref, qt_ref, kf_ref, vt_ref, dtab_ref, g_ref, o_ref,
                 kvar_ref, qvar_ref, flag_ref, acc_ref, m_ref, *, lambda_init):
    tq, tk = Q_TILE, K_TILE
    n_q = qt_ref.shape[2]
    n_k = kf_ref.shape[2] // tk
    n_diag = tq // tk

    kf = kf_ref[0, 0]
    lane = lax.broadcasted_iota(jnp.int32, kf.shape, 1)
    kvar_ref[0] = kf
    kvar_ref[1] = jnp.where(jnp.logical_and(lane >= LANE_ROW, lane < LANE_M1), -kf, kf)
    qt = qt_ref[0, 0]
    row = lax.broadcasted_iota(jnp.int32, qt.shape, 1)
    other2 = jnp.logical_or(jnp.logical_and(row >= QK_DIM, row < HEAD_DIM), row == LANE_M2)
    other1 = jnp.logical_or(row < QK_DIM, row == LANE_M1)
    qvar_ref[0] = jnp.where(other2, jnp.zeros_like(qt), qt)
    qvar_ref[1] = jnp.where(other1, jnp.zeros_like(qt), qt)
    flag_ref[...] = jnp.zeros(flag_ref.shape, F32)

    lv = lam_ref[...]
    lam = (jnp.exp(jnp.sum(lv[0:1] * lv[1:2], axis=1, keepdims=True))
           - jnp.exp(jnp.sum(lv[2:3] * lv[3:4], axis=1, keepdims=True)) + lambda_init)

    def key_tile(qi, k):
        if isinstance(k, int) and k < n_diag:
            return n_diag * qi + k, None, k
        idx = k - n_diag
        after = jnp.where(idx >= n_diag * qi, 1, 0)
        return idx + n_diag * after, after, None

    def shifted_scores(qi, k, br):
        kj, variant, table = key_tile(qi, k)
        rows = pl.ds(pl.multiple_of(kj * tk, tk), tk)
        if table is None:
            kt = kvar_ref[variant, rows, :]
        else:
            kt = kvar_ref[0, rows, :]
            t_lane = lax.broadcasted_iota(jnp.int32, kt.shape, 1)
            kt = jnp.where(jnp.logical_and(t_lane >= LANE_ROW, t_lane < LANE_M1), jnp.zeros_like(kt), kt)
        x = jnp.dot(kt, qvar_ref[br, qi], preferred_element_type=F32)
        if table is not None:
            x = x + dtab_ref[0, table]
        return x, kj

    def write_out(qi, a1, a2):
        o = (a1[0:HEAD_DIM] / a1[HEAD_DIM:HEAD_DIM + 1]
             - lam * (a2[0:HEAD_DIM] / a2[HEAD_DIM:HEAD_DIM + 1]))
        ms = jnp.mean(o * o, axis=0, keepdims=True)
        o_ref[0, 0, qi] = (o * lax.rsqrt(ms + EPS) * g_ref[0]).astype(BF16)

    def fast_query_tile(qi, carry):
        acc = [None, None]
        ahead = [[shifted_scores(qi, k, br) for br in range(2)] for k in range(LOOKAHEAD)]
        for k in range(n_k):
            if k + LOOKAHEAD < n_k:
                ahead.append([shifted_scores(qi, k + LOOKAHEAD, br) for br in range(2)])
            for br, (x, kj) in enumerate(ahead.pop(0)):
                pv = jnp.dot(vt_ref[0, 0, kj], jnp.exp2(x).astype(BF16), preferred_element_type=F32)
                acc[br] = pv if acc[br] is None else acc[br] + pv
        bad = jnp.zeros((1, tq), F32)
        for a in acc:
            den = a[HEAD_DIM:HEAD_DIM + 1]
            mag = jnp.sum(jnp.abs(a), axis=0, keepdims=True)
            ok = jnp.logical_and(jnp.logical_and(den > 1.0 / L_MAX, den < L_MAX), mag < F32_HUGE)
            bad = jnp.maximum(bad, jnp.where(ok, 0.0, 1.0))
        flag_ref[...] = jnp.maximum(flag_ref[...], bad)
        write_out(qi, acc[0], acc[1])
        return carry

    lax.fori_loop(0, n_q, fast_query_tile, 0)

    @pl.when(jnp.max(flag_ref[...]) > 0.0)
    def _safe_path():
        def tile_step(qi, k):
            for br in range(2):
                x, kj = shifted_scores(qi, k, br)
                m_old = m_ref[br]
                m_new = jnp.maximum(m_old, jnp.max(x, axis=0, keepdims=True))
                p = jnp.exp2(x - m_new).astype(BF16)
                pv = jnp.dot(vt_ref[0, 0, kj], p, preferred_element_type=F32)
                acc_ref[br] = jnp.exp2(m_old - m_new) * acc_ref[br] + pv
                m_ref[br] = m_new

        def safe_query_tile(qi, carry):
            m_ref[...] = jnp.full(m_ref.shape, NEG_BIG, F32)
            acc_ref[...] = jnp.zeros(acc_ref.shape, F32)
            for k in range(n_diag):
                tile_step(qi, k)

            def off_diagonal(k, c):
                tile_step(qi, k)
                return c

            lax.fori_loop(n_diag, n_k, off_diagonal, 0)
            write_out(qi, acc_ref[0], acc_ref[1])
            return carry

        lax.fori_loop(0, n_q, safe_query_tile, 0)


def _attention(lam_vecs, qt, kf, vt, dtab, g_sub, lambda_init):
    b_, n_h, s_, _ = kf.shape
    tq, tk = Q_TILE, K_TILE
    n_q, n_k = s_ // tq, s_ // tk
    assert tq % tk == 0
    return pl.pallas_call(
        functools.partial(_attn_kernel, lambda_init=lambda_init),
        grid=(b_, n_h),
        in_specs=[
            pl.BlockSpec((4, QK_DIM), lambda b, h: (0, 0)),
            pl.BlockSpec((1, 1, n_q, LANES, tq), lambda b, h: (b, h, 0, 0, 0)),
            pl.BlockSpec((1, 1, s_, LANES), lambda b, h: (b, h, 0, 0)),
            pl.BlockSpec((1, 1, n_k, V_ROWS, tk), lambda b, h: (b, h, 0, 0, 0)),
            pl.BlockSpec((1, tq // tk, tk, tq), lambda b, h: (h, 0, 0, 0)),
            pl.BlockSpec((1, HEAD_DIM, tq), lambda b, h: (h, 0, 0)),
        ],
        out_specs=pl.BlockSpec((1, 1, n_q, HEAD_DIM, tq), lambda b, h: (b, h, 0, 0, 0)),
        out_shape=jax.ShapeDtypeStruct((b_, n_h, n_q, HEAD_DIM, tq), BF16),
        scratch_shapes=[
            pltpu.VMEM((2, s_, LANES), BF16),
            pltpu.VMEM((2, n_q, LANES, tq), BF16),
            pltpu.VMEM((1, tq), F32),
            pltpu.VMEM((2, V_ROWS, tq), F32),
            pltpu.VMEM((2, 1, tq), F32),
        ],
        compiler_params=pltpu.CompilerParams(
            dimension_semantics=("parallel", "parallel"), vmem_limit_bytes=VMEM_LIMIT),
        name="diff_attn",
    )(lam_vecs, qt, kf, vt, dtab, g_sub)


def _mix_kernel(yat_ref, pc_ref, pprev_ref, pnext_ref, x_ref, mod_ref, wo_ref, wp_ref, ps_ref, cw_ref,
                o_ref, *, seq_len):
    ts = x_ref.shape[1]
    i = pl.program_id(1)
    n_i = pl.num_programs(1)
    has_prev = (i > 0).astype(F32)
    has_next = (i < n_i - 1).astype(F32)

    pc = pc_ref[0]
    prev = pprev_ref[0] * has_prev
    nxt = pnext_ref[0] * has_next
    n_ext = ts + 2 * HALO
    u_ext = jnp.concatenate([prev[:, 0:256], pc[:, 0:256], nxt[:, 0:256]], axis=0)
    cu_ext = jnp.concatenate([prev[:, 512:768], pc[:, 512:768], nxt[:, 512:768]], axis=0)

    a2 = u_ext + pltpu.roll(u_ext, 1, 0)
    a4 = a2 + pltpu.roll(a2, 2, 0)
    a8 = a4 + pltpu.roll(a4, 4, 0)
    a16 = a8 + pltpu.roll(a8, 8, 0)
    lane = lax.broadcasted_iota(jnp.int32, (ts, POOL_WIDTH), 1)
    grp = lane // POOL_GROUP_DIM
    win = []
    for w, a in zip(POOL_WINDOWS, (a2, a4, a8, a16)):
        lead = w // 2 - 1
        r = a if lead == 0 else pltpu.roll(a, n_ext - lead, 0)
        win.append(r[HALO:HALO + ts])
    total = jnp.where(grp == 0, win[0], jnp.where(grp == 1, win[1], jnp.where(grp == 2, win[2], win[3])))
    half = jnp.where(grp == 0, 1, jnp.where(grp == 1, 2, jnp.where(grp == 2, 4, 8)))
    pos = i * ts + lax.broadcasted_iota(jnp.int32, (ts, POOL_WIDTH), 0)
    cnt = (jnp.minimum(pos + half, seq_len) - jnp.maximum(pos - half, 0)).astype(F32)
    diff = total / cnt - pc[:, 0:256]
    y_b = jnp.dot(diff.astype(BF16), wp_ref[...], preferred_element_type=F32) * ps_ref[...]

    cw = cw_ref[...]
    conv = (cw[0:1] * pltpu.roll(cu_ext, 1, 0) + cw[1:2] * cu_ext
            + cw[2:3] * pltpu.roll(cu_ext, n_ext - 1, 0))[HALO:HALO + ts]
    y_c = pc[:, 256:512] * conv

    yat = yat_ref[0, :, 0].reshape(ATTN_WIDTH, ts)
    mixed = lax.dot_general(yat, wo_ref[0:ATTN_WIDTH, :], (((0,), (0,)), ((), ())),
                            preferred_element_type=F32)
    mixed = mixed + jnp.dot(y_b.astype(BF16), wo_ref[ATTN_WIDTH:ATTN_WIDTH + POOL_WIDTH, :],
                            preferred_element_type=F32)
    mixed = mixed + jnp.dot(y_c.astype(BF16), wo_ref[ATTN_WIDTH + POOL_WIDTH:, :],
                            preferred_element_type=F32)
    o_ref[0] = x_ref[0] + mod_ref[0, 2:3, :] * mixed


def _mix(yat, pc, x, mod_l, w_out_bf, w_pool_bd, pool_scale_l, conv_w_l):
    b_, s_, d = x.shape
    ts = PROJ_TS
    assert ts == Q_TILE
    n_t = s_ // ts
    hb = ts // HALO
    n_hb = s_ // HALO
    return pl.pallas_call(
        functools.partial(_mix_kernel, seq_len=s_),
        grid=(b_, n_t),
        in_specs=[
            pl.BlockSpec((1, N_HEADS, 1, HEAD_DIM, ts), lambda b, i: (b, 0, i, 0, 0)),
            pl.BlockSpec((1, ts, 768), lambda b, i: (b, i, 0)),
            pl.BlockSpec((1, HALO, 768), lambda b, i: (b, jnp.maximum(i * hb - 1, 0), 0)),
            pl.BlockSpec((1, HALO, 768), lambda b, i: (b, jnp.minimum((i + 1) * hb, n_hb - 1), 0)),
            pl.BlockSpec((1, ts, d), lambda b, i: (b, i, 0)),
            pl.BlockSpec((1, 6, d), lambda b, i: (b, 0, 0)),
            pl.BlockSpec((d, d), lambda b, i: (0, 0)),
            pl.BlockSpec((POOL_WIDTH, POOL_WIDTH), lambda b, i: (0, 0)),
            pl.BlockSpec((1, POOL_WIDTH), lambda b, i: (0, 0)),
            pl.BlockSpec((3, CONV_WIDTH), lambda b, i: (0, 0)),
        ],
        out_specs=pl.BlockSpec((1, ts, d), lambda b, i: (b, i, 0)),
        out_shape=jax.ShapeDtypeStruct((b_, s_, d), F32),
        compiler_params=pltpu.CompilerParams(
            dimension_semantics=("parallel", "parallel"), vmem_limit_bytes=VMEM_LIMIT),
        name="mix_out_proj",
    )(yat, pc, pc, pc, x, mod_l, w_out_bf, w_pool_bd, pool_scale_l, conv_w_l)


def _ffn_kernel(x_ref, mod_ref, g_ref, wgu_ref, wd_ref, gf_ref, o_ref, *, final_norm):
    x = x_ref[0]
    ms = jnp.mean(x * x, axis=-1, keepdims=True)
    xn = x * lax.rsqrt(ms + EPS) * g_ref[...]
    h = (xn * (1.0 + mod_ref[0, 4:5, :]) + mod_ref[0, 3:4, :]).astype(BF16)
    gate = jnp.dot(h, wgu_ref[:, 0:D_FF], preferred_element_type=F32)
    up = jnp.dot(h, wgu_ref[:, D_FF:2 * D_FF], preferred_element_type=F32)
    act = (gate * jax.nn.sigmoid(gate) * up).astype(BF16)
    y = x + mod_ref[0, 5:6, :] * jnp.dot(act, wd_ref[...], preferred_element_type=F32)
    if final_norm:
        ms2 = jnp.mean(y * y, axis=-1, keepdims=True)
        y = y * lax.rsqrt(ms2 + EPS) * gf_ref[...]
    o_ref[0] = y


def _ffn(x, mod_l, g_ffn_l, w_gu_bf, w_down_bf, g_final, final_norm):
    b_, s_, d = x.shape
    ts = FFN_TS
    n_t = s_ // ts
    resident = pl.Buffered(1)
    return pl.pallas_call(
        functools.partial(_ffn_kernel, final_norm=final_norm),
        grid=(b_, n_t),
        in_specs=[
            pl.BlockSpec((1, ts, d), lambda b, i: (b, i, 0)),
            pl.BlockSpec((1, 6, d), lambda b, i: (b, 0, 0)),
            pl.BlockSpec((1, d), lambda b, i: (0, 0)),
            pl.BlockSpec((d, 2 * D_FF), lambda b, i: (0, 0), pipeline_mode=resident),
            pl.BlockSpec((D_FF, d), lambda b, i: (0, 0), pipeline_mode=resident),
            pl.BlockSpec((1, d), lambda b, i: (0, 0)),
        ],
        out_specs=pl.BlockSpec((1, ts, d), lambda b, i: (b, i, 0)),
        out_shape=jax.ShapeDtypeStruct((b_, s_, d), F32),
        compiler_params=pltpu.CompilerParams(
            dimension_semantics=("parallel", "parallel"), vmem_limit_bytes=VMEM_LIMIT),
        name="swiglu_ffn",
    )(x, mod_l, g_ffn_l, w_gu_bf, w_down_bf, g_final)


def _attention_constants():
    slopes = _slopes()
    tq, tk = Q_TILE, K_TILE
    v_const = np.zeros((V_ROWS - HEAD_DIM, tk), np.float32)
    v_const[0, :] = 1.0
    key_pos = np.arange(tq).reshape(tq // tk, tk)
    dist = np.abs(key_pos[:, :, None] - np.arange(tq)[None, None, :]).astype(np.float64)
    dtab = (-slopes[:, None, None, None] * LOG2E * dist[None]).astype(np.float32)
    return jnp.asarray(v_const, dtype=BF16), jnp.asarray(dtab)


def kernel(x, c, w_ada, b_ada, g_mix, w_in, lambda_q1, lambda_k1, lambda_q2, lambda_k2, g_subln, w_pool,
           pool_scale, conv_w, w_out, g_ffn, w_gate_up, w_down, g_final):
    b_, s_, d = x.shape
    v_const, dtab = _attention_constants()
    mod = _modulation(c, w_ada, b_ada).reshape(DEPTH, b_, 6, d)
    g_final2 = g_final.reshape(1, d)
    for l in range(DEPTH):
        lambda_init = 0.8 - 0.6 * math.exp(-0.3 * l)
        lam_vecs = jnp.stack([lambda_q1[l], lambda_k1[l], lambda_q2[l], lambda_k2[l]]).astype(F32)
        g_sub = jnp.broadcast_to(
            (g_subln[l].astype(F32) * (1.0 - lambda_init)).reshape(N_HEADS, HEAD_DIM, 1),
            (N_HEADS, HEAD_DIM, Q_TILE))
        w_pool_bd = jax.scipy.linalg.block_diag(*[w_pool[l, g] for g in range(len(POOL_WINDOWS))]).astype(BF16)

        qt, kf, vt, pc = _projection(x, mod[l], g_mix[l].reshape(1, d), w_in[l].astype(BF16), v_const)
        yat = _attention(lam_vecs, qt, kf, vt, dtab, g_sub, lambda_init)
        x = _mix(yat, pc, x, mod[l], w_out[l].astype(BF16), w_pool_bd,
                 pool_scale[l].reshape(1, POOL_WIDTH), conv_w[l])
        x = _ffn(x, mod[l], g_ffn[l].reshape(1, d), w_gate_up[l].astype(BF16), w_down[l].astype(BF16),
                 g_final2, final_norm=(l == DEPTH - 1))
    return x
```

```python
import functools
import math

import numpy as np
import jax
import jax.numpy as jnp
from jax import lax
from jax.experimental import pallas as pl
from jax.experimental.pallas import tpu as pltpu

F32 = jnp.float32
BF16 = jnp.bfloat16

D_MODEL = 1024
DEPTH = 2
N_HEADS = 8
HEAD_DIM = 64
QK_DIM = 32
ATTN_WIDTH = N_HEADS * HEAD_DIM
POOL_WIDTH = 256
POOL_WINDOWS = (2, 4, 8, 16)
POOL_GROUP_DIM = 64
CONV_WIDTH = 256
IN_WIDTH = 3 * ATTN_WIDTH + POOL_WIDTH + 3 * CONV_WIDTH
D_FF = 2816
EPS = 1e-6
LOG2E = math.log2(math.e)
QK_SCALE = QK_DIM ** -0.5

LANES = 128
Q_TILE = 512
K_TILE = 256
V_ROWS = 80
HALO = 16
PROJ_TS = 512
FFN_TS = 512
VMEM_LIMIT = 56 * 1024 * 1024
NEG_BIG = -1e30
L_MAX = 2.0 ** 100
F32_HUGE = 3.0e38
LOOKAHEAD = 1
Q_PER_TRIP = 2
Q_SUB = 256

N_PIECES = 3
LANE_ROW = HEAD_DIM
LANE_COL = LANE_ROW + N_PIECES
LANE_M1 = LANE_COL + N_PIECES
LANE_M2 = LANE_M1 + 1
LANE_END = LANE_M2 + 1


def _slopes():
    return np.array([2.0 ** (-8.0 * (h + 1) / N_HEADS) for h in range(N_HEADS)], dtype=np.float64)


def _mod_kernel(c_ref, w_ref, b_ref, o_ref):
    c = c_ref[...]
    ca = c * jax.nn.sigmoid(c)
    o_ref[0] = jnp.dot(ca, w_ref[0], preferred_element_type=F32) + b_ref[0]


def _modulation(c, w_ada, b_ada):
    b_, d = c.shape
    n6 = w_ada.shape[-1] // d
    return pl.pallas_call(
        _mod_kernel,
        grid=(DEPTH, n6),
        in_specs=[
            pl.BlockSpec((b_, d), lambda l, n: (0, 0)),
            pl.BlockSpec((1, d, d), lambda l, n: (l, 0, n)),
            pl.BlockSpec((1, 1, d), lambda l, n: (l, 0, n)),
        ],
        out_specs=pl.BlockSpec((1, b_, d), lambda l, n: (l, 0, n)),
        out_shape=jax.ShapeDtypeStruct((DEPTH, b_, n6 * d), F32),
        compiler_params=pltpu.CompilerParams(
            dimension_semantics=("arbitrary", "arbitrary"), vmem_limit_bytes=VMEM_LIMIT),
        name="adaln_mod",
    )(c, w_ada, b_ada.reshape(DEPTH, 1, -1))


def _round_bf16(v):
    return v.astype(BF16).astype(F32)


def _pieces(v):
    out = []
    for _ in range(N_PIECES):
        p = _round_bf16(v)
        out.append(p)
        v = v - p
    return out


def _place(lane, first_lane, pieces, rest):
    for n, p in enumerate(pieces):
        rest = jnp.where(lane == first_lane + n, p, rest)
    return rest


def _proj_kernel(x_ref, mod_ref, g_ref, w_ref, vc_ref, qt_ref, kf_ref, vt_ref, pc_ref, *, seq_len):
    ts = x_ref.shape[1]
    x = x_ref[0]
    ms = jnp.mean(x * x, axis=-1, keepdims=True)
    xn = x * lax.rsqrt(ms + EPS) * g_ref[...]
    h = (xn * (1.0 + mod_ref[0, 1:2, :]) + mod_ref[0, 0:1, :]).astype(BF16)
    proj = jnp.dot(h, w_ref[...], preferred_element_type=F32)

    lane = lax.broadcasted_iota(jnp.int32, (ts, LANES), 1)
    row = lax.broadcasted_iota(jnp.int32, (ts, LANES), 0)
    pos = (pl.program_id(1) * ts + row - seq_len // 2).astype(F32) * LOG2E
    pos_pieces = _pieces(pos)
    zeros = jnp.zeros((ts, LANES), F32)
    q_pos = _place(lane, LANE_ROW, [-p for p in pos_pieces], zeros)
    k_pos = _place(lane, LANE_COL, pos_pieces, zeros)
    q_ones = jnp.where(jnp.logical_and(lane >= LANE_COL, lane < LANE_M1), 1.0, 0.0)
    k_ones = jnp.where(jnp.logical_or(jnp.logical_and(lane >= LANE_ROW, lane < LANE_COL),
                                      jnp.logical_and(lane >= LANE_M1, lane < LANE_END)), 1.0, 0.0)
    in1 = lane < QK_DIM
    in2 = jnp.logical_and(lane >= QK_DIM, lane < HEAD_DIM)
    data_lane = lane < HEAD_DIM

    n_kt = ts // K_TILE
    for c in range(N_HEADS // 2):
        qc = proj[:, c * LANES:(c + 1) * LANES] * (QK_SCALE * LOG2E)
        kc = proj[:, ATTN_WIDTH + c * LANES:ATTN_WIDTH + (c + 1) * LANES]
        vc = proj[:, 2 * ATTN_WIDTH + c * LANES:2 * ATTN_WIDTH + (c + 1) * LANES]
        pairs = ((qc, kc), (pltpu.roll(qc, HEAD_DIM, 1), pltpu.roll(kc, HEAD_DIM, 1)))
        for e, (qq, kk) in enumerate(pairs):
            hd = 2 * c + e
            slope = float(_slopes()[hd])
            own = qq * kk
            own1 = jnp.sum(jnp.where(in1, own, 0.0), axis=1, keepdims=True)
            own2 = jnp.sum(jnp.where(in2, own, 0.0), axis=1, keepdims=True)
            q_bias = jnp.where(lane == LANE_M1, -own1, jnp.where(lane == LANE_M2, -own2, q_pos * slope + q_ones))
            q_full = jnp.where(data_lane, qq, q_bias)
            qt_ref[0, hd, 0] = q_full.T.astype(BF16)
            kf_ref[0, hd] = jnp.where(data_lane, kk, k_pos * slope + k_ones).astype(BF16)
        vt = vc.T
        for e in range(2):
            hd = 2 * c + e
            for t in range(n_kt):
                vt_ref[0, hd, t, 0:HEAD_DIM, :] = (
                    vt[e * HEAD_DIM:(e + 1) * HEAD_DIM, t * K_TILE:(t + 1) * K_TILE].astype(BF16))
                vt_ref[0, hd, t, HEAD_DIM:V_ROWS, :] = vc_ref[...]

    o_p = 3 * ATTN_WIDTH
    pc_ref[0, :, 0:256] = proj[:, o_p:o_p + 256]
    pc_ref[0, :, 256:512] = proj[:, o_p + 256:o_p + 512]
    pc_ref[0, :, 512:768] = proj[:, o_p + 512:o_p + 768] * proj[:, o_p + 768:o_p + 1024]


def _projection(x, mod_l, g_mix_l, w_in_bf, v_const):
    b_, s_, d = x.shape
    ts = PROJ_TS
    assert ts == Q_TILE
    n_t = s_ // ts
    n_kt = ts // K_TILE
    return pl.pallas_call(
        functools.partial(_proj_kernel, seq_len=s_),
        grid=(b_, n_t),
        in_specs=[
            pl.BlockSpec((1, ts, d), lambda b, i: (b, i, 0)),
            pl.BlockSpec((1, 6, d), lambda b, i: (b, 0, 0)),
            pl.BlockSpec((1, d), lambda b, i: (0, 0)),
            pl.BlockSpec((d, IN_WIDTH), lambda b, i: (0, 0)),
            pl.BlockSpec((V_ROWS - HEAD_DIM, K_TILE), lambda b, i: (0, 0)),
        ],
        out_specs=[
            pl.BlockSpec((1, N_HEADS, 1, LANES, ts), lambda b, i: (b, 0, i, 0, 0)),
            pl.BlockSpec((1, N_HEADS, ts, LANES), lambda b, i: (b, 0, i, 0)),
            pl.BlockSpec((1, N_HEADS, n_kt, V_ROWS, K_TILE), lambda b, i: (b, 0, i, 0, 0)),
            pl.BlockSpec((1, ts, 768), lambda b, i: (b, i, 0)),
        ],
        out_shape=[
            jax.ShapeDtypeStruct((b_, N_HEADS, n_t, LANES, ts), BF16),
            jax.ShapeDtypeStruct((b_, N_HEADS, s_, LANES), BF16),
            jax.ShapeDtypeStruct((b_, N_HEADS, s_ // K_TILE, V_ROWS, K_TILE), BF16),
            jax.ShapeDtypeStruct((b_, s_, 768), F32),
        ],
        compiler_params=pltpu.CompilerParams(
            dimension_semantics=("parallel", "parallel"), vmem_limit_bytes=VMEM_LIMIT),
        name="in_proj",
    )(x, mod_l, g_mix_l, w_in_bf, v_const)


def _attn_kernel(lam_ref, qt_ref, kf_ref, vt_ref, dtab_ref, g_ref, o_ref,
                 kvar_ref, qvar_ref, flag_ref, acc_ref, m_ref, *, lambda_init):
    tq, tk = Q_TILE, K_TILE
    n_q = qt_ref.shape[2]
    n_k = kf_ref.shape[2] // tk
    n_diag = tq // tk

    kf = kf_ref[0, 0]
    lane = lax.broadcasted_iota(jnp.int32, kf.shape, 1)
    kvar_ref[0] = kf
    kvar_ref[1] = jnp.where(jnp.logical_and(lane >= LANE_ROW, lane < LANE_M1), -kf, kf)
    qt = qt_ref[0, 0]
    row = lax.broadcasted_iota(jnp.int32, qt.shape, 1)
    other2 = jnp.logical_or(jnp.logical_and(row >= QK_DIM, row < HEAD_DIM), row == LANE_M2)
    other1 = jnp.logical_or(row < QK_DIM, row == LANE_M1)
    qvar_ref[0] = jnp.where(other2, jnp.zeros_like(qt), qt)
    qvar_ref[1] = jnp.where(other1, jnp.zeros_like(qt), qt)
    flag_ref[...] = jnp.zeros(flag_ref.shape, F32)

    lv = lam_ref[...]
    lam = (jnp.exp(jnp.sum(lv[0:1] * lv[1:2], axis=1, keepdims=True))
           - jnp.exp(jnp.sum(lv[2:3] * lv[3:4], axis=1, keepdims=True)) + lambda_init)

    def key_tile(qi, k):
        if isinstance(k, int) and k < n_diag:
            return n_diag * qi + k, None, k
        idx = k - n_diag
        after = jnp.where(idx >= n_diag * qi, 1, 0)
        return idx + n_diag * after, after, None

    def shifted_scores(qi, k, br, cols=slice(None)):
        kj, variant, table = key_tile(qi, k)
        rows = pl.ds(pl.multiple_of(kj * tk, tk), tk)
        if table is None:
            kt = kvar_ref[variant, rows, :]
        else:
            kt = kvar_ref[0, rows, :]
            t_lane = lax.broadcasted_iota(jnp.int32, kt.shape, 1)
            kt = jnp.where(jnp.logical_and(t_lane >= LANE_ROW, t_lane < LANE_M1), jnp.zeros_like(kt), kt)
        x = jnp.dot(kt, qvar_ref[br, qi, :, cols], preferred_element_type=F32)
        if table is not None:
            x = x + dtab_ref[0, table, :, cols]
        return x, kj

    def write_out(qi, a1, a2):
        o = (a1[0:HEAD_DIM] / a1[HEAD_DIM:HEAD_DIM + 1]
             - lam * (a2[0:HEAD_DIM] / a2[HEAD_DIM:HEAD_DIM + 1]))
        ms = jnp.mean(o * o, axis=0, keepdims=True)
        o_ref[0, 0, qi] = (o * lax.rsqrt(ms + EPS) * g_ref[0]).astype(BF16)

    def check_and_write(qi, acc):
        acc = [jnp.concatenate(a, axis=1) for a in acc]
        bad = jnp.zeros((1, tq), F32)
        for a in acc:
            den = a[HEAD_DIM:HEAD_DIM + 1]
            mag = jnp.sum(jnp.abs(a), axis=0, keepdims=True)
            ok = jnp.logical_and(jnp.logical_and(den > 1.0 / L_MAX, den < L_MAX), mag < F32_HUGE)
            bad = jnp.maximum(bad, jnp.where(ok, 0.0, 1.0))
        flag_ref[...] = jnp.maximum(flag_ref[...], bad)
        write_out(qi, acc[0], acc[1])

    n_sub = tq // Q_SUB
    sub_cols = [slice(j * Q_SUB, (j + 1) * Q_SUB) for j in range(n_sub)]

    def fast_trip(i, carry):
        steps = [(Q_PER_TRIP * i + j, k) for j in range(Q_PER_TRIP) for k in range(n_k)]

        def scores(step, cols):
            return [shifted_scores(step[0], step[1], br, cols) for br in range(2)]

        ahead = [[scores(st, cols) for cols in sub_cols] for st in steps[:LOOKAHEAD]]
        acc = [[None] * n_sub, [None] * n_sub]
        for n, (qi, k) in enumerate(steps):
            cur = ahead.pop(0)
            nxt = []
            for j, cols in enumerate(sub_cols):
                if n + LOOKAHEAD < len(steps):
                    nxt.append(scores(steps[n + LOOKAHEAD], cols))
                for br, (x, kj) in enumerate(cur[j]):
                    pv = jnp.dot(vt_ref[0, 0, kj], jnp.exp2(x).astype(BF16), preferred_element_type=F32)
                    acc[br][j] = pv if k == 0 else acc[br][j] + pv
            if nxt:
                ahead.append(nxt)
            if k == n_k - 1:
                check_and_write(qi, acc)
        return carry

    lax.fori_loop(0, n_q // Q_PER_TRIP, fast_trip, 0)

    @pl.when(jnp.max(flag_ref[...]) > 0.0)
    def _safe_path():
        def tile_step(qi, k):
            for br in range(2):
                x, kj = shifted_scores(qi, k, br)
                m_old = m_ref[br]
                m_new = jnp.maximum(m_old, jnp.max(x, axis=0, keepdims=True))
                p = jnp.exp2(x - m_new).astype(BF16)
                pv = jnp.dot(vt_ref[0, 0, kj], p, preferred_element_type=F32)
                acc_ref[br] = jnp.exp2(m_old - m_new) * acc_ref[br] + pv
                m_ref[br] = m_new

        def safe_query_tile(qi, carry):
            m_ref[...] = jnp.full(m_ref.shape, NEG_BIG, F32)
            acc_ref[...] = jnp.zeros(acc_ref.shape, F32)
            for k in range(n_diag):
                tile_step(qi, k)

            def off_diagonal(k, c):
                tile_step(qi, k)
                return c

            lax.fori_loop(n_diag, n_k, off_diagonal, 0)
            write_out(qi, acc_ref[0], acc_ref[1])
            return carry

        lax.fori_loop(0, n_q, safe_query_tile, 0)


def _attention(lam_vecs, qt, kf, vt, dtab, g_sub, lambda_init):
    b_, n_h, s_, _ = kf.shape
    tq, tk = Q_TILE, K_TILE
    n_q, n_k = s_ // tq, s_ // tk
    assert tq % tk == 0
    return pl.pallas_call(
        functools.partial(_attn_kernel, lambda_init=lambda_init),
        grid=(b_, n_h),
        in_specs=[
            pl.BlockSpec((4, QK_DIM), lambda b, h: (0, 0)),
            pl.BlockSpec((1, 1, n_q, LANES, tq), lambda b, h: (b, h, 0, 0, 0)),
            pl.BlockSpec((1, 1, s_, LANES), lambda b, h: (b, h, 0, 0)),
            pl.BlockSpec((1, 1, n_k, V_ROWS, tk), lambda b, h: (b, h, 0, 0, 0)),
            pl.BlockSpec((1, tq // tk, tk, tq), lambda b, h: (h, 0, 0, 0)),
            pl.BlockSpec((1, HEAD_DIM, tq), lambda b, h: (h, 0, 0)),
        ],
        out_specs=pl.BlockSpec((1, 1, n_q, HEAD_DIM, tq), lambda b, h: (b, h, 0, 0, 0)),
        out_shape=jax.ShapeDtypeStruct((b_, n_h, n_q, HEAD_DIM, tq), BF16),
        scratch_shapes=[
            pltpu.VMEM((2, s_, LANES), BF16),
            pltpu.VMEM((2, n_q, LANES, tq), BF16),
            pltpu.VMEM((1, tq), F32),
            pltpu.VMEM((2, V_ROWS, tq), F32),
            pltpu.VMEM((2, 1, tq), F32),
        ],
        compiler_params=pltpu.CompilerParams(
            dimension_semantics=("parallel", "parallel"), vmem_limit_bytes=VMEM_LIMIT),
        name="diff_attn",
    )(lam_vecs, qt, kf, vt, dtab, g_sub)


def _mix_kernel(yat_ref, pc_ref, pprev_ref, pnext_ref, x_ref, mod_ref, wo_ref, wp_ref, ps_ref, cw_ref,
                o_ref, *, seq_len):
    ts = x_ref.shape[1]
    i = pl.program_id(1)
    n_i = pl.num_programs(1)
    has_prev = (i > 0).astype(F32)
    has_next = (i < n_i - 1).astype(F32)

    pc = pc_ref[0]
    prev = pprev_ref[0] * has_prev
    nxt = pnext_ref[0] * has_next
    n_ext = ts + 2 * HALO
    u_ext = jnp.concatenate([prev[:, 0:256], pc[:, 0:256], nxt[:, 0:256]], axis=0)
    cu_ext = jnp.concatenate([prev[:, 512:768], pc[:, 512:768], nxt[:, 512:768]], axis=0)

    a2 = u_ext + pltpu.roll(u_ext, 1, 0)
    a4 = a2 + pltpu.roll(a2, 2, 0)
    a8 = a4 + pltpu.roll(a4, 4, 0)
    a16 = a8 + pltpu.roll(a8, 8, 0)
    lane = lax.broadcasted_iota(jnp.int32, (ts, POOL_WIDTH), 1)
    grp = lane // POOL_GROUP_DIM
    win = []
    for w, a in zip(POOL_WINDOWS, (a2, a4, a8, a16)):
        lead = w // 2 - 1
        r = a if lead == 0 else pltpu.roll(a, n_ext - lead, 0)
        win.append(r[HALO:HALO + ts])
    total = jnp.where(grp == 0, win[0], jnp.where(grp == 1, win[1], jnp.where(grp == 2, win[2], win[3])))
    half = jnp.where(grp == 0, 1, jnp.where(grp == 1, 2, jnp.where(grp == 2, 4, 8)))
    pos = i * ts + lax.broadcasted_iota(jnp.int32, (ts, POOL_WIDTH), 0)
    cnt = (jnp.minimum(pos + half, seq_len) - jnp.maximum(pos - half, 0)).astype(F32)
    diff = total / cnt - pc[:, 0:256]
    y_b = jnp.dot(diff.astype(BF16), wp_ref[...], preferred_element_type=F32) * ps_ref[...]

    cw = cw_ref[...]
    conv = (cw[0:1] * pltpu.roll(cu_ext, 1, 0) + cw[1:2] * cu_ext
            + cw[2:3] * pltpu.roll(cu_ext, n_ext - 1, 0))[HALO:HALO + ts]
    y_c = pc[:, 256:512] * conv

    yat = yat_ref[0, :, 0].reshape(ATTN_WIDTH, ts)
    mixed = lax.dot_general(yat, wo_ref[0:ATTN_WIDTH, :], (((0,), (0,)), ((), ())),
                            preferred_element_type=F32)
    mixed = mixed + jnp.dot(y_b.astype(BF16), wo_ref[ATTN_WIDTH:ATTN_WIDTH + POOL_WIDTH, :],
                            preferred_element_type=F32)
    mixed = mixed + jnp.dot(y_c.astype(BF16), wo_ref[ATTN_WIDTH + POOL_WIDTH:, :],
                            preferred_element_type=F32)
    o_ref[0] = x_ref[0] + mod_ref[0, 2:3, :] * mixed


def _mix(yat, pc, x, mod_l, w_out_bf, w_pool_bd, pool_scale_l, conv_w_l):
    b_, s_, d = x.shape
    ts = PROJ_TS
    assert ts == Q_TILE
    n_t = s_ // ts
    hb = ts // HALO
    n_hb = s_ // HALO
    return pl.pallas_call(
        functools.partial(_mix_kernel, seq_len=s_),
        grid=(b_, n_t),
        in_specs=[
            pl.BlockSpec((1, N_HEADS, 1, HEAD_DIM, ts), lambda b, i: (b, 0, i, 0, 0)),
            pl.BlockSpec((1, ts, 768), lambda b, i: (b, i, 0)),
            pl.BlockSpec((1, HALO, 768), lambda b, i: (b, jnp.maximum(i * hb - 1, 0), 0)),
            pl.BlockSpec((1, HALO, 768), lambda b, i: (b, jnp.minimum((i + 1) * hb, n_hb - 1), 0)),
            pl.BlockSpec((1, ts, d), lambda b, i: (b, i, 0)),
            pl.BlockSpec((1, 6, d), lambda b, i: (b, 0, 0)),
            pl.BlockSpec((d, d), lambda b, i: (0, 0)),
            pl.BlockSpec((POOL_WIDTH, POOL_WIDTH), lambda b, i: (0, 0)),
            pl.BlockSpec((1, POOL_WIDTH), lambda b, i: (0, 0)),
            pl.BlockSpec((3, CONV_WIDTH), lambda b, i: (0, 0)),
        ],
        out_specs=pl.BlockSpec((1, ts, d), lambda b, i: (b, i, 0)),
        out_shape=jax.ShapeDtypeStruct((b_, s_, d), F32),
        compiler_params=pltpu.CompilerParams(
            dimension_semantics=("parallel", "parallel"), vmem_limit_bytes=VMEM_LIMIT),
        name="mix_out_proj",
    )(yat, pc, pc, pc, x, mod_l, w_out_bf, w_pool_bd, pool_scale_l, conv_w_l)


def _ffn_kernel(x_ref, mod_ref, g_ref, wgu_ref, wd_ref, gf_ref, o_ref, *, final_norm):
    x = x_ref[0]
    ms = jnp.mean(x * x, axis=-1, keepdims=True)
    xn = x * lax.rsqrt(ms + EPS) * g_ref[...]
    h = (xn * (1.0 + mod_ref[0, 4:5, :]) + mod_ref[0, 3:4, :]).astype(BF16)
    gate = jnp.dot(h, wgu_ref[:, 0:D_FF], preferred_element_type=F32)
    up = jnp.dot(h, wgu_ref[:, D_FF:2 * D_FF], preferred_element_type=F32)
    act = (gate * jax.nn.sigmoid(gate) * up).astype(BF16)
    y = x + mod_ref[0, 5:6, :] * jnp.dot(act, wd_ref[...], preferred_element_type=F32)
    if final_norm:
        ms2 = jnp.mean(y * y, axis=-1, keepdims=True)
        y = y * lax.rsqrt(ms2 + EPS) * gf_ref[...]
    o_ref[0] = y


def _ffn(x, mod_l, g_ffn_l, w_gu_bf, w_down_bf, g_final, final_norm):
    b_, s_, d = x.shape
    ts = FFN_TS
    n_t = s_ // ts
    resident = pl.Buffered(1)
    return pl.pallas_call(
        functools.partial(_ffn_kernel, final_norm=final_norm),
        grid=(b_, n_t),
        in_specs=[
            pl.BlockSpec((1, ts, d), lambda b, i: (b, i, 0)),
            pl.BlockSpec((1, 6, d), lambda b, i: (b, 0, 0)),
            pl.BlockSpec((1, d), lambda b, i: (0, 0)),
            pl.BlockSpec((d, 2 * D_FF), lambda b, i: (0, 0), pipeline_mode=resident),
            pl.BlockSpec((D_FF, d), lambda b, i: (0, 0), pipeline_mode=resident),
            pl.BlockSpec((1, d), lambda b, i: (0, 0)),
        ],
        out_specs=pl.BlockSpec((1, ts, d), lambda b, i: (b, i, 0)),
        out_shape=jax.ShapeDtypeStruct((b_, s_, d), F32),
        compiler_params=pltpu.CompilerParams(
            dimension_semantics=("parallel", "parallel"), vmem_limit_bytes=VMEM_LIMIT),
        name="swiglu_ffn",
    )(x, mod_l, g_ffn_l, w_gu_bf, w_down_bf, g_final)


def _attention_constants():
    slopes = _slopes()
    tq, tk = Q_TILE, K_TILE
    v_const = np.zeros((V_ROWS - HEAD_DIM, tk), np.float32)
    v_const[0, :] = 1.0
    key_pos = np.arange(tq).reshape(tq // tk, tk)
    dist = np.abs(key_pos[:, :, None] - np.arange(tq)[None, None, :]).astype(np.float64)
    dtab = (-slopes[:, None, None, None] * LOG2E * dist[None]).astype(np.float32)
    return jnp.asarray(v_const, dtype=BF16), jnp.asarray(dtab)


def kernel(x, c, w_ada, b_ada, g_mix, w_in, lambda_q1, lambda_k1, lambda_q2, lambda_k2, g_subln, w_pool,
           pool_scale, conv_w, w_out, g_ffn, w_gate_up, w_down, g_final):
    b_, s_, d = x.shape
    v_const, dtab = _attention_constants()
    mod = _modulation(c, w_ada, b_ada).reshape(DEPTH, b_, 6, d)
    g_final2 = g_final.reshape(1, d)
    for l in range(DEPTH):
        lambda_init = 0.8 - 0.6 * math.exp(-0.3 * l)
        lam_vecs = jnp.stack([lambda_q1[l], lambda_k1[l], lambda_q2[l], lambda_k2[l]]).astype(F32)
        g_sub = jnp.broadcast_to(
            (g_subln[l].astype(F32) * (1.0 - lambda_init)).reshape(N_HEADS, HEAD_DIM, 1),
            (N_HEADS, HEAD_DIM, Q_TILE))
        w_pool_bd = jax.scipy.linalg.block_diag(*[w_pool[l, g] for g in range(len(POOL_WINDOWS))]).astype(BF16)

        qt, kf, vt, pc = _projection(x, mod[l], g_mix[l].reshape(1, d), w_in[l].astype(BF16), v_const)
        yat = _attention(lam_vecs, qt, kf, vt, dtab, g_sub, lambda_init)
        x = _mix(yat, pc, x, mod[l], w_out[l].astype(BF16), w_pool_bd,
                 pool_scale[l].reshape(1, POOL_WIDTH), conv_w[l])
        x = _ffn(x, mod[l], g_ffn[l].reshape(1, d), w_gate_up[l].astype(BF16), w_down[l].astype(BF16),
                 g_final2, final_norm=(l == DEPTH - 1))
    return x
```

```python
import functools
import math

import numpy as np
import jax
import jax.numpy as jnp
from jax import lax
from jax.experimental import pallas as pl
from jax.experimental.pallas import tpu as pltpu

F32 = jnp.float32
BF16 = jnp.bfloat16

D_MODEL = 1024
DEPTH = 2
N_HEADS = 8
HEAD_DIM = 64
QK_DIM = 32
ATTN_WIDTH = N_HEADS * HEAD_DIM
POOL_WIDTH = 256
POOL_WINDOWS = (2, 4, 8, 16)
POOL_GROUP_DIM = 64
CONV_WIDTH = 256
IN_WIDTH = 3 * ATTN_WIDTH + POOL_WIDTH + 3 * CONV_WIDTH
D_FF = 2816
EPS = 1e-6
LOG2E = math.log2(math.e)
QK_SCALE = QK_DIM ** -0.5

LANES = 128
Q_TILE = 512
K_TILE = 256
SUBLANES = 8
HALO = 16
PROJ_TS = 512
FFN_TS = 512
VMEM_LIMIT = 56 * 1024 * 1024
NEG_BIG = -1e30
L_MAX = 2.0 ** 100
F32_HUGE = 3.0e38
LOOKAHEAD = 1
Q_PER_TRIP = 2
Q_SUB = 256

N_PIECES = 3
LANE_ROW = HEAD_DIM
LANE_COL = LANE_ROW + N_PIECES
LANE_M1 = LANE_COL + N_PIECES
LANE_M2 = LANE_M1 + 1
LANE_END = LANE_M2 + 1


def _slopes():
    return np.array([2.0 ** (-8.0 * (h + 1) / N_HEADS) for h in range(N_HEADS)], dtype=np.float64)


def _mod_kernel(c_ref, w_ref, b_ref, o_ref):
    c = c_ref[...]
    ca = c * jax.nn.sigmoid(c)
    o_ref[0] = jnp.dot(ca, w_ref[0], preferred_element_type=F32) + b_ref[0]


def _modulation(c, w_ada, b_ada):
    b_, d = c.shape
    n6 = w_ada.shape[-1] // d
    return pl.pallas_call(
        _mod_kernel,
        grid=(DEPTH, n6),
        in_specs=[
            pl.BlockSpec((b_, d), lambda l, n: (0, 0)),
            pl.BlockSpec((1, d, d), lambda l, n: (l, 0, n)),
            pl.BlockSpec((1, 1, d), lambda l, n: (l, 0, n)),
        ],
        out_specs=pl.BlockSpec((1, b_, d), lambda l, n: (l, 0, n)),
        out_shape=jax.ShapeDtypeStruct((DEPTH, b_, n6 * d), F32),
        compiler_params=pltpu.CompilerParams(
            dimension_semantics=("arbitrary", "arbitrary"), vmem_limit_bytes=VMEM_LIMIT),
        name="adaln_mod",
    )(c, w_ada, b_ada.reshape(DEPTH, 1, -1))


def _round_bf16(v):
    return v.astype(BF16).astype(F32)


def _pieces(v):
    out = []
    for _ in range(N_PIECES):
        p = _round_bf16(v)
        out.append(p)
        v = v - p
    return out


def _place(lane, first_lane, pieces, rest):
    for n, p in enumerate(pieces):
        rest = jnp.where(lane == first_lane + n, p, rest)
    return rest


def _proj_kernel(x_ref, mod_ref, g_ref, w_ref, qt_ref, kf_ref, vt_ref, pc_ref, *, seq_len):
    ts = x_ref.shape[1]
    x = x_ref[0]
    ms = jnp.mean(x * x, axis=-1, keepdims=True)
    xn = x * lax.rsqrt(ms + EPS) * g_ref[...]
    h = (xn * (1.0 + mod_ref[0, 1:2, :]) + mod_ref[0, 0:1, :]).astype(BF16)
    proj = jnp.dot(h, w_ref[...], preferred_element_type=F32)

    lane = lax.broadcasted_iota(jnp.int32, (ts, LANES), 1)
    row = lax.broadcasted_iota(jnp.int32, (ts, LANES), 0)
    pos = (pl.program_id(1) * ts + row - seq_len // 2).astype(F32) * LOG2E
    pos_pieces = _pieces(pos)
    zeros = jnp.zeros((ts, LANES), F32)
    q_pos = _place(lane, LANE_ROW, [-p for p in pos_pieces], zeros)
    k_pos = _place(lane, LANE_COL, pos_pieces, zeros)
    q_ones = jnp.where(jnp.logical_and(lane >= LANE_COL, lane < LANE_M1), 1.0, 0.0)
    k_ones = jnp.where(jnp.logical_or(jnp.logical_and(lane >= LANE_ROW, lane < LANE_COL),
                                      jnp.logical_and(lane >= LANE_M1, lane < LANE_END)), 1.0, 0.0)
    in1 = lane < QK_DIM
    in2 = jnp.logical_and(lane >= QK_DIM, lane < HEAD_DIM)
    data_lane = lane < HEAD_DIM

    n_kt = ts // K_TILE
    for c in range(N_HEADS // 2):
        qc = proj[:, c * LANES:(c + 1) * LANES] * (QK_SCALE * LOG2E)
        kc = proj[:, ATTN_WIDTH + c * LANES:ATTN_WIDTH + (c + 1) * LANES]
        vc = proj[:, 2 * ATTN_WIDTH + c * LANES:2 * ATTN_WIDTH + (c + 1) * LANES]
        pairs = ((qc, kc), (pltpu.roll(qc, HEAD_DIM, 1), pltpu.roll(kc, HEAD_DIM, 1)))
        for e, (qq, kk) in enumerate(pairs):
            hd = 2 * c + e
            slope = float(_slopes()[hd])
            own = qq * kk
            own1 = jnp.sum(jnp.where(in1, own, 0.0), axis=1, keepdims=True)
            own2 = jnp.sum(jnp.where(in2, own, 0.0), axis=1, keepdims=True)
            q_bias = jnp.where(lane == LANE_M1, -own1, jnp.where(lane == LANE_M2, -own2, q_pos * slope + q_ones))
            q_full = jnp.where(data_lane, qq, q_bias)
            qt_ref[0, hd, 0] = q_full.T.astype(BF16)
            kf_ref[0, hd] = jnp.where(data_lane, kk, k_pos * slope + k_ones).astype(BF16)
        vt = vc.T
        for e in range(2):
            hd = 2 * c + e
            for t in range(n_kt):
                vt_ref[0, hd, t] = vt[e * HEAD_DIM:(e + 1) * HEAD_DIM, t * K_TILE:(t + 1) * K_TILE].astype(BF16)

    o_p = 3 * ATTN_WIDTH
    pc_ref[0, :, 0:256] = proj[:, o_p:o_p + 256]
    pc_ref[0, :, 256:512] = proj[:, o_p + 256:o_p + 512]
    pc_ref[0, :, 512:768] = proj[:, o_p + 512:o_p + 768] * proj[:, o_p + 768:o_p + 1024]


def _projection(x, mod_l, g_mix_l, w_in_bf):
    b_, s_, d = x.shape
    ts = PROJ_TS
    assert ts == Q_TILE
    n_t = s_ // ts
    n_kt = ts // K_TILE
    return pl.pallas_call(
        functools.partial(_proj_kernel, seq_len=s_),
        grid=(b_, n_t),
        in_specs=[
            pl.BlockSpec((1, ts, d), lambda b, i: (b, i, 0)),
            pl.BlockSpec((1, 6, d), lambda b, i: (b, 0, 0)),
            pl.BlockSpec((1, d), lambda b, i: (0, 0)),
            pl.BlockSpec((d, IN_WIDTH), lambda b, i: (0, 0)),
        ],
        out_specs=[
            pl.BlockSpec((1, N_HEADS, 1, LANES, ts), lambda b, i: (b, 0, i, 0, 0)),
            pl.BlockSpec((1, N_HEADS, ts, LANES), lambda b, i: (b, 0, i, 0)),
            pl.BlockSpec((1, N_HEADS, n_kt, HEAD_DIM, K_TILE), lambda b, i: (b, 0, i, 0, 0)),
            pl.BlockSpec((1, ts, 768), lambda b, i: (b, i, 0)),
        ],
        out_shape=[
            jax.ShapeDtypeStruct((b_, N_HEADS, n_t, LANES, ts), BF16),
            jax.ShapeDtypeStruct((b_, N_HEADS, s_, LANES), BF16),
            jax.ShapeDtypeStruct((b_, N_HEADS, s_ // K_TILE, HEAD_DIM, K_TILE), BF16),
            jax.ShapeDtypeStruct((b_, s_, 768), F32),
        ],
        compiler_params=pltpu.CompilerParams(
            dimension_semantics=("parallel", "parallel"), vmem_limit_bytes=VMEM_LIMIT),
        name="in_proj",
    )(x, mod_l, g_mix_l, w_in_bf)


def _attn_kernel(lam_ref, qt_ref, kf_ref, vt_ref, dtab_ref, g_ref, o_ref,
                 kvar_ref, flag_ref, acc_ref, den_ref, m_ref, *, lambda_init):
    tq, tk = Q_TILE, K_TILE
    n_q = qt_ref.shape[2]
    n_k = kf_ref.shape[2] // tk
    n_diag = tq // tk

    kf = kf_ref[0, 0]
    lane = lax.broadcasted_iota(jnp.int32, kf.shape, 1)
    kvar_ref[0] = kf
    kvar_ref[1] = jnp.where(jnp.logical_and(lane >= LANE_ROW, lane < LANE_M1), -kf, kf)
    flag_ref[...] = jnp.zeros(flag_ref.shape, F32)

    lv = lam_ref[...]
    lam = (jnp.exp(jnp.sum(lv[0:1] * lv[1:2], axis=1, keepdims=True))
           - jnp.exp(jnp.sum(lv[2:3] * lv[3:4], axis=1, keepdims=True)) + lambda_init)

    def key_tile(qi, k):
        if isinstance(k, int) and k < n_diag:
            return n_diag * qi + k, None, k
        idx = k - n_diag
        after = jnp.where(idx >= n_diag * qi, 1, 0)
        return idx + n_diag * after, after, None

    def key_operand(qi, k, br):
        kj, variant, table = key_tile(qi, k)
        kt = kvar_ref[0 if table is not None else variant, pl.ds(pl.multiple_of(kj * tk, tk), tk), :]
        t_lane = lax.broadcasted_iota(jnp.int32, kt.shape, 1)
        if br == 0:
            drop = jnp.logical_or(jnp.logical_and(t_lane >= QK_DIM, t_lane < HEAD_DIM), t_lane == LANE_M2)
        else:
            drop = jnp.logical_or(t_lane < QK_DIM, t_lane == LANE_M1)
        if table is not None:
            drop = jnp.logical_or(drop, jnp.logical_and(t_lane >= LANE_ROW, t_lane < LANE_M1))
        return jnp.where(drop, jnp.zeros_like(kt), kt), kj, table

    def shifted_scores(qi, operand, cols=slice(None)):
        kt, kj, table = operand
        x = jnp.dot(kt, qt_ref[0, 0, qi, :, cols], preferred_element_type=F32)
        if table is not None:
            x = x + dtab_ref[0, table, :, cols]
        return x, kj

    def write_out(qi, a1, d1, a2, d2):
        o = a1 / d1 - lam * (a2 / d2)
        ms = jnp.mean(o * o, axis=0, keepdims=True)
        o_ref[0, 0, qi] = (o * lax.rsqrt(ms + EPS) * g_ref[0]).astype(BF16)

    def check_and_write(qi, acc, den):
        acc = [jnp.concatenate(a, axis=1) for a in acc]
        den = [jnp.sum(jnp.concatenate(d, axis=1), axis=0, keepdims=True) for d in den]
        bad = jnp.zeros((1, tq), F32)
        for a, d in zip(acc, den):
            mag = jnp.sum(jnp.abs(a), axis=0, keepdims=True)
            ok = jnp.logical_and(jnp.logical_and(d > 1.0 / L_MAX, d < L_MAX), mag < F32_HUGE)
            bad = jnp.maximum(bad, jnp.where(ok, 0.0, 1.0))
        flag_ref[...] = jnp.maximum(flag_ref[...], bad)
        write_out(qi, acc[0], den[0], acc[1], den[1])

    n_sub = tq // Q_SUB
    sub_cols = [slice(j * Q_SUB, (j + 1) * Q_SUB) for j in range(n_sub)]

    def fast_trip(i, carry):
        steps = [(Q_PER_TRIP * i + j, k) for j in range(Q_PER_TRIP) for k in range(n_k)]

        def scores(step):
            operands = [key_operand(step[0], step[1], br) for br in range(2)]
            return [[shifted_scores(step[0], op, cols) for op in operands] for cols in sub_cols]

        ahead = [scores(st) for st in steps[:LOOKAHEAD]]
        acc = [[None] * n_sub, [None] * n_sub]
        den = [[None] * n_sub, [None] * n_sub]
        for n, (qi, k) in enumerate(steps):
            cur = ahead.pop(0)
            nxt = []
            operands = None
            if n + LOOKAHEAD < len(steps):
                nq, nk = steps[n + LOOKAHEAD]
                operands = [key_operand(nq, nk, br) for br in range(2)]
            for j, cols in enumerate(sub_cols):
                if operands is not None:
                    nxt.append([shifted_scores(nq, op, cols) for op in operands])
                for br, (x, kj) in enumerate(cur[j]):
                    p = jnp.exp2(x)
                    pv = jnp.dot(vt_ref[0, 0, kj], p.astype(BF16), preferred_element_type=F32)
                    ps = jnp.sum(p.reshape(tk // SUBLANES, SUBLANES, Q_SUB), axis=0)
                    acc[br][j] = pv if k == 0 else acc[br][j] + pv
                    den[br][j] = ps if k == 0 else den[br][j] + ps
            if nxt:
                ahead.append(nxt)
            if k == n_k - 1:
                check_and_write(qi, acc, den)
        return carry

    lax.fori_loop(0, n_q // Q_PER_TRIP, fast_trip, 0)

    @pl.when(jnp.max(flag_ref[...]) > 0.0)
    def _safe_path():
        def tile_step(qi, k):
            for br in range(2):
                x, kj = shifted_scores(qi, key_operand(qi, k, br))
                m_old = m_ref[br]
                m_new = jnp.maximum(m_old, jnp.max(x, axis=0, keepdims=True))
                alpha = jnp.exp2(m_old - m_new)
                p = jnp.exp2(x - m_new)
                pv = jnp.dot(vt_ref[0, 0, kj], p.astype(BF16), preferred_element_type=F32)
                acc_ref[br] = alpha * acc_ref[br] + pv
                den_ref[br] = alpha * den_ref[br] + jnp.sum(p, axis=0, keepdims=True)
                m_ref[br] = m_new

        def safe_query_tile(qi, carry):
            m_ref[...] = jnp.full(m_ref.shape, NEG_BIG, F32)
            acc_ref[...] = jnp.zeros(acc_ref.shape, F32)
            den_ref[...] = jnp.zeros(den_ref.shape, F32)
            for k in range(n_diag):
                tile_step(qi, k)

            def off_diagonal(k, c):
                tile_step(qi, k)
                return c

            lax.fori_loop(n_diag, n_k, off_diagonal, 0)
            write_out(qi, acc_ref[0], den_ref[0], acc_ref[1], den_ref[1])
            return carry

        lax.fori_loop(0, n_q, safe_query_tile, 0)


def _attention(lam_vecs, qt, kf, vt, dtab, g_sub, lambda_init):
    b_, n_h, s_, _ = kf.shape
    tq, tk = Q_TILE, K_TILE
    n_q, n_k = s_ // tq, s_ // tk
    assert tq % tk == 0
    return pl.pallas_call(
        functools.partial(_attn_kernel, lambda_init=lambda_init),
        grid=(b_, n_h),
        in_specs=[
            pl.BlockSpec((4, QK_DIM), lambda b, h: (0, 0)),
            pl.BlockSpec((1, 1, n_q, LANES, tq), lambda b, h: (b, h, 0, 0, 0)),
            pl.BlockSpec((1, 1, s_, LANES), lambda b, h: (b, h, 0, 0)),
            pl.BlockSpec((1, 1, n_k, HEAD_DIM, tk), lambda b, h: (b, h, 0, 0, 0)),
            pl.BlockSpec((1, tq // tk, tk, tq), lambda b, h: (h, 0, 0, 0)),
            pl.BlockSpec((1, HEAD_DIM, tq), lambda b, h: (h, 0, 0)),
        ],
        out_specs=pl.BlockSpec((1, 1, n_q, HEAD_DIM, tq), lambda b, h: (b, h, 0, 0, 0)),
        out_shape=jax.ShapeDtypeStruct((b_, n_h, n_q, HEAD_DIM, tq), BF16),
        scratch_shapes=[
            pltpu.VMEM((2, s_, LANES), BF16),
            pltpu.VMEM((1, tq), F32),
            pltpu.VMEM((2, HEAD_DIM, tq), F32),
            pltpu.VMEM((2, 1, tq), F32),
            pltpu.VMEM((2, 1, tq), F32),
        ],
        compiler_params=pltpu.CompilerParams(
            dimension_semantics=("parallel", "parallel"), vmem_limit_bytes=VMEM_LIMIT),
        name="diff_attn",
    )(lam_vecs, qt, kf, vt, dtab, g_sub)


def _mix_kernel(yat_ref, pc_ref, pprev_ref, pnext_ref, x_ref, mod_ref, wo_ref, wp_ref, ps_ref, cw_ref,
                o_ref, *, seq_len):
    ts = x_ref.shape[1]
    i = pl.program_id(1)
    n_i = pl.num_programs(1)
    has_prev = (i > 0).astype(F32)
    has_next = (i < n_i - 1).astype(F32)

    pc = pc_ref[0]
    prev = pprev_ref[0] * has_prev
    nxt = pnext_ref[0] * has_next
    n_ext = ts + 2 * HALO
    u_ext = jnp.concatenate([prev[:, 0:256], pc[:, 0:256], nxt[:, 0:256]], axis=0)
    cu_ext = jnp.concatenate([prev[:, 512:768], pc[:, 512:768], nxt[:, 512:768]], axis=0)

    a2 = u_ext + pltpu.roll(u_ext, 1, 0)
    a4 = a2 + pltpu.roll(a2, 2, 0)
    a8 = a4 + pltpu.roll(a4, 4, 0)
    a16 = a8 + pltpu.roll(a8, 8, 0)
    lane = lax.broadcasted_iota(jnp.int32, (ts, POOL_WIDTH), 1)
    grp = lane // POOL_GROUP_DIM
    win = []
    for w, a in zip(POOL_WINDOWS, (a2, a4, a8, a16)):
        lead = w // 2 - 1
        r = a if lead == 0 else pltpu.roll(a, n_ext - lead, 0)
        win.append(r[HALO:HALO + ts])
    total = jnp.where(grp == 0, win[0], jnp.where(grp == 1, win[1], jnp.where(grp == 2, win[2], win[3])))
    half = jnp.where(grp == 0, 1, jnp.where(grp == 1, 2, jnp.where(grp == 2, 4, 8)))
    pos = i * ts + lax.broadcasted_iota(jnp.int32, (ts, POOL_WIDTH), 0)
    cnt = (jnp.minimum(pos + half, seq_len) - jnp.maximum(pos - half, 0)).astype(F32)
    diff = total / cnt - pc[:, 0:256]
    y_b = jnp.dot(diff.astype(BF16), wp_ref[...], preferred_element_type=F32) * ps_ref[...]

    cw = cw_ref[...]
    conv = (cw[0:1] * pltpu.roll(cu_ext, 1, 0) + cw[1:2] * cu_ext
            + cw[2:3] * pltpu.roll(cu_ext, n_ext - 1, 0))[HALO:HALO + ts]
    y_c = pc[:, 256:512] * conv

    yat = yat_ref[0, :, 0].reshape(ATTN_WIDTH, ts)
    mixed = lax.dot_general(yat, wo_ref[0:ATTN_WIDTH, :], (((0,), (0,)), ((), ())),
                            preferred_element_type=F32)
    mixed = mixed + jnp.dot(y_b.astype(BF16), wo_ref[ATTN_WIDTH:ATTN_WIDTH + POOL_WIDTH, :],
                            preferred_element_type=F32)
    mixed = mixed + jnp.dot(y_c.astype(BF16), wo_ref[ATTN_WIDTH + POOL_WIDTH:, :],
                            preferred_element_type=F32)
    o_ref[0] = x_ref[0] + mod_ref[0, 2:3, :] * mixed


def _mix(yat, pc, x, mod_l, w_out_bf, w_pool_bd, pool_scale_l, conv_w_l):
    b_, s_, d = x.shape
    ts = PROJ_TS
    assert ts == Q_TILE
    n_t = s_ // ts
    hb = ts // HALO
    n_hb = s_ // HALO
    return pl.pallas_call(
        functools.partial(_mix_kernel, seq_len=s_),
        grid=(b_, n_t),
        in_specs=[
            pl.BlockSpec((1, N_HEADS, 1, HEAD_DIM, ts), lambda b, i: (b, 0, i, 0, 0)),
            pl.BlockSpec((1, ts, 768), lambda b, i: (b, i, 0)),
            pl.BlockSpec((1, HALO, 768), lambda b, i: (b, jnp.maximum(i * hb - 1, 0), 0)),
            pl.BlockSpec((1, HALO, 768), lambda b, i: (b, jnp.minimum((i + 1) * hb, n_hb - 1), 0)),
            pl.BlockSpec((1, ts, d), lambda b, i: (b, i, 0)),
            pl.BlockSpec((1, 6, d), lambda b, i: (b, 0, 0)),
            pl.BlockSpec((d, d), lambda b, i: (0, 0)),
            pl.BlockSpec((POOL_WIDTH, POOL_WIDTH), lambda b, i: (0, 0)),
            pl.BlockSpec((1, POOL_WIDTH), lambda b, i: (0, 0)),
            pl.BlockSpec((3, CONV_WIDTH), lambda b, i: (0, 0)),
        ],
        out_specs=pl.BlockSpec((1, ts, d), lambda b, i: (b, i, 0)),
        out_shape=jax.ShapeDtypeStruct((b_, s_, d), F32),
        compiler_params=pltpu.CompilerParams(
            dimension_semantics=("parallel", "parallel"), vmem_limit_bytes=VMEM_LIMIT),
        name="mix_out_proj",
    )(yat, pc, pc, pc, x, mod_l, w_out_bf, w_pool_bd, pool_scale_l, conv_w_l)


def _ffn_kernel(x_ref, mod_ref, g_ref, wgu_ref, wd_ref, gf_ref, o_ref, *, final_norm):
    x = x_ref[0]
    ms = jnp.mean(x * x, axis=-1, keepdims=True)
    xn = x * lax.rsqrt(ms + EPS) * g_ref[...]
    h = (xn * (1.0 + mod_ref[0, 4:5, :]) + mod_ref[0, 3:4, :]).astype(BF16)
    gate = jnp.dot(h, wgu_ref[:, 0:D_FF], preferred_element_type=F32)
    up = jnp.dot(h, wgu_ref[:, D_FF:2 * D_FF], preferred_element_type=F32)
    act = (gate * jax.nn.sigmoid(gate) * up).astype(BF16)
    y = x + mod_ref[0, 5:6, :] * jnp.dot(act, wd_ref[...], preferred_element_type=F32)
    if final_norm:
        ms2 = jnp.mean(y * y, axis=-1, keepdims=True)
        y = y * lax.rsqrt(ms2 + EPS) * gf_ref[...]
    o_ref[0] = y


def _ffn(x, mod_l, g_ffn_l, w_gu_bf, w_down_bf, g_final, final_norm):
    b_, s_, d = x.shape
    ts = FFN_TS
    n_t = s_ // ts
    resident = pl.Buffered(1)
    return pl.pallas_call(
        functools.partial(_ffn_kernel, final_norm=final_norm),
        grid=(b_, n_t),
        in_specs=[
            pl.BlockSpec((1, ts, d), lambda b, i: (b, i, 0)),
            pl.BlockSpec((1, 6, d), lambda b, i: (b, 0, 0)),
            pl.BlockSpec((1, d), lambda b, i: (0, 0)),
            pl.BlockSpec((d, 2 * D_FF), lambda b, i: (0, 0), pipeline_mode=resident),
            pl.BlockSpec((D_FF, d), lambda b, i: (0, 0), pipeline_mode=resident),
            pl.BlockSpec((1, d), lambda b, i: (0, 0)),
        ],
        out_specs=pl.BlockSpec((1, ts, d), lambda b, i: (b, i, 0)),
        out_shape=jax.ShapeDtypeStruct((b_, s_, d), F32),
        compiler_params=pltpu.CompilerParams(
            dimension_semantics=("parallel", "parallel"), vmem_limit_bytes=VMEM_LIMIT),
        name="swiglu_ffn",
    )(x, mod_l, g_ffn_l, w_gu_bf, w_down_bf, g_final)


def _diagonal_bias():
    tq, tk = Q_TILE, K_TILE
    key_pos = np.arange(tq).reshape(tq // tk, tk)
    dist = np.abs(key_pos[:, :, None] - np.arange(tq)[None, None, :]).astype(np.float64)
    return jnp.asarray((-_slopes()[:, None, None, None] * LOG2E * dist[None]).astype(np.float32))


def kernel(x, c, w_ada, b_ada, g_mix, w_in, lambda_q1, lambda_k1, lambda_q2, lambda_k2, g_subln, w_pool,
           pool_scale, conv_w, w_out, g_ffn, w_gate_up, w_down, g_final):
    b_, s_, d = x.shape
    dtab = _diagonal_bias()
    mod = _modulation(c, w_ada, b_ada).reshape(DEPTH, b_, 6, d)
    g_final2 = g_final.reshape(1, d)
    for l in range(DEPTH):
        lambda_init = 0.8 - 0.6 * math.exp(-0.3 * l)
        lam_vecs = jnp.stack([lambda_q1[l], lambda_k1[l], lambda_q2[l], lambda_k2[l]]).astype(F32)
        g_sub = jnp.broadcast_to(
            (g_subln[l].astype(F32) * (1.0 - lambda_init)).reshape(N_HEADS, HEAD_DIM, 1),
            (N_HEADS, HEAD_DIM, Q_TILE))
        w_pool_bd = jax.scipy.linalg.block_diag(*[w_pool[l, g] for g in range(len(POOL_WINDOWS))]).astype(BF16)

        qt, kf, vt, pc = _projection(x, mod[l], g_mix[l].reshape(1, d), w_in[l].astype(BF16))
        yat = _attention(lam_vecs, qt, kf, vt, dtab, g_sub, lambda_init)
        x = _mix(yat, pc, x, mod[l], w_out[l].astype(BF16), w_pool_bd,
                 pool_scale[l].reshape(1, POOL_WIDTH), conv_w[l])
        x = _ffn(x, mod[l], g_ffn[l].reshape(1, d), w_gate_up[l].astype(BF16), w_down[l].astype(BF16),
                 g_final2, final_norm=(l == DEPTH - 1))
    return x
```

```python
import functools
import math

import numpy as np
import jax
import jax.numpy as jnp
from jax import lax
from jax.experimental import pallas as pl
from jax.experimental.pallas import tpu as pltpu

F32 = jnp.float32
BF16 = jnp.bfloat16

D_MODEL = 1024
DEPTH = 2
N_HEADS = 8
HEAD_DIM = 64
QK_DIM = 32
ATTN_WIDTH = N_HEADS * HEAD_DIM
POOL_WIDTH = 256
POOL_WINDOWS = (2, 4, 8, 16)
POOL_GROUP_DIM = 64
CONV_WIDTH = 256
IN_WIDTH = 3 * ATTN_WIDTH + POOL_WIDTH + 3 * CONV_WIDTH
D_FF = 2816
EPS = 1e-6
LOG2E = math.log2(math.e)
QK_SCALE = QK_DIM ** -0.5

LANES = 128
Q_TILE = 512
K_TILE = 256
SUBLANES = 8
HALO = 16
PROJ_TS = 512
FFN_TS = 512
VMEM_LIMIT = 56 * 1024 * 1024
NEG_BIG = -1e30
L_MAX = 2.0 ** 100
F32_HUGE = 3.0e38
LOOKAHEAD = 1
Q_PER_TRIP = 2
Q_SUB = 256

N_PIECES = 3
LANE_ROW = HEAD_DIM
LANE_COL = LANE_ROW + N_PIECES
LANE_END = LANE_COL + N_PIECES


def _slopes():
    return np.array([2.0 ** (-8.0 * (h + 1) / N_HEADS) for h in range(N_HEADS)], dtype=np.float64)


def _mod_kernel(c_ref, w_ref, b_ref, o_ref):
    c = c_ref[...]
    ca = c * jax.nn.sigmoid(c)
    o_ref[0] = jnp.dot(ca, w_ref[0], preferred_element_type=F32) + b_ref[0]


def _modulation(c, w_ada, b_ada):
    b_, d = c.shape
    n6 = w_ada.shape[-1] // d
    return pl.pallas_call(
        _mod_kernel,
        grid=(DEPTH, n6),
        in_specs=[
            pl.BlockSpec((b_, d), lambda l, n: (0, 0)),
            pl.BlockSpec((1, d, d), lambda l, n: (l, 0, n)),
            pl.BlockSpec((1, 1, d), lambda l, n: (l, 0, n)),
        ],
        out_specs=pl.BlockSpec((1, b_, d), lambda l, n: (l, 0, n)),
        out_shape=jax.ShapeDtypeStruct((DEPTH, b_, n6 * d), F32),
        compiler_params=pltpu.CompilerParams(
            dimension_semantics=("arbitrary", "arbitrary"), vmem_limit_bytes=VMEM_LIMIT),
        name="adaln_mod",
    )(c, w_ada, b_ada.reshape(DEPTH, 1, -1))


def _round_bf16(v):
    return v.astype(BF16).astype(F32)


def _pieces(v):
    out = []
    for _ in range(N_PIECES):
        p = _round_bf16(v)
        out.append(p)
        v = v - p
    return out


def _place(lane, first_lane, pieces, rest):
    for n, p in enumerate(pieces):
        rest = jnp.where(lane == first_lane + n, p, rest)
    return rest


def _proj_kernel(x_ref, mod_ref, g_ref, w_ref, qt_ref, kf_ref, vt_ref, pc_ref, *, seq_len):
    ts = x_ref.shape[1]
    x = x_ref[0]
    ms = jnp.mean(x * x, axis=-1, keepdims=True)
    xn = x * lax.rsqrt(ms + EPS) * g_ref[...]
    h = (xn * (1.0 + mod_ref[0, 1:2, :]) + mod_ref[0, 0:1, :]).astype(BF16)
    proj = jnp.dot(h, w_ref[...], preferred_element_type=F32)

    lane = lax.broadcasted_iota(jnp.int32, (ts, LANES), 1)
    row = lax.broadcasted_iota(jnp.int32, (ts, LANES), 0)
    pos = (pl.program_id(1) * ts + row - seq_len // 2).astype(F32) * LOG2E
    pos_pieces = _pieces(pos)
    zeros = jnp.zeros((ts, LANES), F32)
    q_pos = _place(lane, LANE_ROW, [-p for p in pos_pieces], zeros)
    k_pos = _place(lane, LANE_COL, pos_pieces, zeros)
    q_ones = jnp.where(jnp.logical_and(lane >= LANE_COL, lane < LANE_END), 1.0, 0.0)
    k_ones = jnp.where(jnp.logical_and(lane >= LANE_ROW, lane < LANE_COL), 1.0, 0.0)
    data_lane = lane < HEAD_DIM

    n_kt = ts // K_TILE
    for c in range(N_HEADS // 2):
        qc = proj[:, c * LANES:(c + 1) * LANES] * (QK_SCALE * LOG2E)
        kc = proj[:, ATTN_WIDTH + c * LANES:ATTN_WIDTH + (c + 1) * LANES]
        vc = proj[:, 2 * ATTN_WIDTH + c * LANES:2 * ATTN_WIDTH + (c + 1) * LANES]
        pairs = ((qc, kc), (pltpu.roll(qc, HEAD_DIM, 1), pltpu.roll(kc, HEAD_DIM, 1)))
        for e, (qq, kk) in enumerate(pairs):
            hd = 2 * c + e
            slope = float(_slopes()[hd])
            q_full = jnp.where(data_lane, qq, q_pos * slope + q_ones)
            qt_ref[0, hd, 0] = q_full.T.astype(BF16)
            kf_ref[0, hd] = jnp.where(data_lane, kk, k_pos * slope + k_ones).astype(BF16)
        vt = vc.T
        for e in range(2):
            hd = 2 * c + e
            for t in range(n_kt):
                vt_ref[0, hd, t] = vt[e * HEAD_DIM:(e + 1) * HEAD_DIM, t * K_TILE:(t + 1) * K_TILE].astype(BF16)

    o_p = 3 * ATTN_WIDTH
    pc_ref[0, :, 0:256] = proj[:, o_p:o_p + 256]
    pc_ref[0, :, 256:512] = proj[:, o_p + 256:o_p + 512]
    pc_ref[0, :, 512:768] = proj[:, o_p + 512:o_p + 768] * proj[:, o_p + 768:o_p + 1024]


def _projection(x, mod_l, g_mix_l, w_in_bf):
    b_, s_, d = x.shape
    ts = PROJ_TS
    assert ts == Q_TILE
    n_t = s_ // ts
    n_kt = ts // K_TILE
    return pl.pallas_call(
        functools.partial(_proj_kernel, seq_len=s_),
        grid=(b_, n_t),
        in_specs=[
            pl.BlockSpec((1, ts, d), lambda b, i: (b, i, 0)),
            pl.BlockSpec((1, 6, d), lambda b, i: (b, 0, 0)),
            pl.BlockSpec((1, d), lambda b, i: (0, 0)),
            pl.BlockSpec((d, IN_WIDTH), lambda b, i: (0, 0)),
        ],
        out_specs=[
            pl.BlockSpec((1, N_HEADS, 1, LANES, ts), lambda b, i: (b, 0, i, 0, 0)),
            pl.BlockSpec((1, N_HEADS, ts, LANES), lambda b, i: (b, 0, i, 0)),
            pl.BlockSpec((1, N_HEADS, n_kt, HEAD_DIM, K_TILE), lambda b, i: (b, 0, i, 0, 0)),
            pl.BlockSpec((1, ts, 768), lambda b, i: (b, i, 0)),
        ],
        out_shape=[
            jax.ShapeDtypeStruct((b_, N_HEADS, n_t, LANES, ts), BF16),
            jax.ShapeDtypeStruct((b_, N_HEADS, s_, LANES), BF16),
            jax.ShapeDtypeStruct((b_, N_HEADS, s_ // K_TILE, HEAD_DIM, K_TILE), BF16),
            jax.ShapeDtypeStruct((b_, s_, 768), F32),
        ],
        compiler_params=pltpu.CompilerParams(
            dimension_semantics=("parallel", "parallel"), vmem_limit_bytes=VMEM_LIMIT),
        name="in_proj",
    )(x, mod_l, g_mix_l, w_in_bf)


def _attn_kernel(lam_ref, qt_ref, kf_ref, vt_ref, dtab_ref, g_ref, o_ref,
                 kvar_ref, flag_ref, acc_ref, den_ref, m_ref, *, lambda_init):
    tq, tk = Q_TILE, K_TILE
    n_q = qt_ref.shape[2]
    n_k = kf_ref.shape[2] // tk
    n_diag = tq // tk

    kf = kf_ref[0, 0]
    lane = lax.broadcasted_iota(jnp.int32, kf.shape, 1)
    kvar_ref[0] = kf
    kvar_ref[1] = jnp.where(jnp.logical_and(lane >= LANE_ROW, lane < LANE_END), -kf, kf)
    flag_ref[...] = jnp.zeros(flag_ref.shape, F32)

    lv = lam_ref[...]
    lam = (jnp.exp(jnp.sum(lv[0:1] * lv[1:2], axis=1, keepdims=True))
           - jnp.exp(jnp.sum(lv[2:3] * lv[3:4], axis=1, keepdims=True)) + lambda_init)

    def key_tile(qi, k):
        if isinstance(k, int) and k < n_diag:
            return n_diag * qi + k, None, k
        idx = k - n_diag
        after = jnp.where(idx >= n_diag * qi, 1, 0)
        return idx + n_diag * after, after, None

    def key_operand(qi, k, br):
        kj, variant, table = key_tile(qi, k)
        kt = kvar_ref[0 if table is not None else variant, pl.ds(pl.multiple_of(kj * tk, tk), tk), :]
        t_lane = lax.broadcasted_iota(jnp.int32, kt.shape, 1)
        if br == 0:
            drop = jnp.logical_and(t_lane >= QK_DIM, t_lane < HEAD_DIM)
        else:
            drop = t_lane < QK_DIM
        if table is not None:
            drop = jnp.logical_or(drop, jnp.logical_and(t_lane >= LANE_ROW, t_lane < LANE_END))
        return jnp.where(drop, jnp.zeros_like(kt), kt), kj, table

    def shifted_scores(qi, operand, cols=slice(None)):
        kt, kj, table = operand
        x = jnp.dot(kt, qt_ref[0, 0, qi, :, cols], preferred_element_type=F32)
        if table is not None:
            x = x + dtab_ref[0, table, :, cols]
        return x, kj

    def write_out(qi, a1, d1, a2, d2):
        o = a1 / d1 - lam * (a2 / d2)
        ms = jnp.mean(o * o, axis=0, keepdims=True)
        o_ref[0, 0, qi] = (o * lax.rsqrt(ms + EPS) * g_ref[0]).astype(BF16)

    def check_and_write(qi, acc, den):
        acc = [jnp.concatenate(a, axis=1) for a in acc]
        den = [jnp.sum(jnp.concatenate(d, axis=1), axis=0, keepdims=True) for d in den]
        bad = jnp.zeros((1, tq), F32)
        for a, d in zip(acc, den):
            mag = jnp.sum(jnp.abs(a), axis=0, keepdims=True)
            ok = jnp.logical_and(jnp.logical_and(d > 1.0 / L_MAX, d < L_MAX), mag < F32_HUGE)
            bad = jnp.maximum(bad, jnp.where(ok, 0.0, 1.0))
        flag_ref[...] = jnp.maximum(flag_ref[...], bad)
        write_out(qi, acc[0], den[0], acc[1], den[1])

    n_sub = tq // Q_SUB
    sub_cols = [slice(j * Q_SUB, (j + 1) * Q_SUB) for j in range(n_sub)]

    def fast_trip(i, carry):
        steps = [(Q_PER_TRIP * i + j, k) for j in range(Q_PER_TRIP) for k in range(n_k)]

        def scores(step):
            operands = [key_operand(step[0], step[1], br) for br in range(2)]
            return [[shifted_scores(step[0], op, cols) for op in operands] for cols in sub_cols]

        ahead = [scores(st) for st in steps[:LOOKAHEAD]]
        acc = [[None] * n_sub, [None] * n_sub]
        den = [[None] * n_sub, [None] * n_sub]
        for n, (qi, k) in enumerate(steps):
            cur = ahead.pop(0)
            nxt = []
            operands = None
            if n + LOOKAHEAD < len(steps):
                nq, nk = steps[n + LOOKAHEAD]
                operands = [key_operand(nq, nk, br) for br in range(2)]
            for j, cols in enumerate(sub_cols):
                if operands is not None:
                    nxt.append([shifted_scores(nq, op, cols) for op in operands])
                for br, (x, kj) in enumerate(cur[j]):
                    p = jnp.exp2(x)
                    pv = jnp.dot(vt_ref[0, 0, kj], p.astype(BF16), preferred_element_type=F32)
                    ps = jnp.sum(p.reshape(tk // SUBLANES, SUBLANES, Q_SUB), axis=0)
                    acc[br][j] = pv if k == 0 else acc[br][j] + pv
                    den[br][j] = ps if k == 0 else den[br][j] + ps
            if nxt:
                ahead.append(nxt)
            if k == n_k - 1:
                check_and_write(qi, acc, den)
        return carry

    lax.fori_loop(0, n_q // Q_PER_TRIP, fast_trip, 0)

    @pl.when(jnp.max(flag_ref[...]) > 0.0)
    def _safe_path():
        def tile_step(qi, k):
            for br in range(2):
                x, kj = shifted_scores(qi, key_operand(qi, k, br))
                m_old = m_ref[br]
                m_new = jnp.maximum(m_old, jnp.max(x, axis=0, keepdims=True))
                alpha = jnp.exp2(m_old - m_new)
                p = jnp.exp2(x - m_new)
                pv = jnp.dot(vt_ref[0, 0, kj], p.astype(BF16), preferred_element_type=F32)
                acc_ref[br] = alpha * acc_ref[br] + pv
                den_ref[br] = alpha * den_ref[br] + jnp.sum(p, axis=0, keepdims=True)
                m_ref[br] = m_new

        def safe_query_tile(qi, carry):
            m_ref[...] = jnp.full(m_ref.shape, NEG_BIG, F32)
            acc_ref[...] = jnp.zeros(acc_ref.shape, F32)
            den_ref[...] = jnp.zeros(den_ref.shape, F32)
            for k in range(n_diag):
                tile_step(qi, k)

            def off_diagonal(k, c):
                tile_step(qi, k)
                return c

            lax.fori_loop(n_diag, n_k, off_diagonal, 0)
            write_out(qi, acc_ref[0], den_ref[0], acc_ref[1], den_ref[1])
            return carry

        lax.fori_loop(0, n_q, safe_query_tile, 0)


def _attention(lam_vecs, qt, kf, vt, dtab, g_sub, lambda_init):
    b_, n_h, s_, _ = kf.shape
    tq, tk = Q_TILE, K_TILE
    n_q, n_k = s_ // tq, s_ // tk
    assert tq % tk == 0
    return pl.pallas_call(
        functools.partial(_attn_kernel, lambda_init=lambda_init),
        grid=(b_, n_h),
        in_specs=[
            pl.BlockSpec((4, QK_DIM), lambda b, h: (0, 0)),
            pl.BlockSpec((1, 1, n_q, LANES, tq), lambda b, h: (b, h, 0, 0, 0)),
            pl.BlockSpec((1, 1, s_, LANES), lambda b, h: (b, h, 0, 0)),
            pl.BlockSpec((1, 1, n_k, HEAD_DIM, tk), lambda b, h: (b, h, 0, 0, 0)),
            pl.BlockSpec((1, tq // tk, tk, tq), lambda b, h: (h, 0, 0, 0)),
            pl.BlockSpec((1, HEAD_DIM, tq), lambda b, h: (h, 0, 0)),
        ],
        out_specs=pl.BlockSpec((1, 1, n_q, HEAD_DIM, tq), lambda b, h: (b, h, 0, 0, 0)),
        out_shape=jax.ShapeDtypeStruct((b_, n_h, n_q, HEAD_DIM, tq), BF16),
        scratch_shapes=[
            pltpu.VMEM((2, s_, LANES), BF16),
            pltpu.VMEM((1, tq), F32),
            pltpu.VMEM((2, HEAD_DIM, tq), F32),
            pltpu.VMEM((2, 1, tq), F32),
            pltpu.VMEM((2, 1, tq), F32),
        ],
        compiler_params=pltpu.CompilerParams(
            dimension_semantics=("parallel", "parallel"), vmem_limit_bytes=VMEM_LIMIT),
        name="diff_attn",
    )(lam_vecs, qt, kf, vt, dtab, g_sub)


def _mix_kernel(yat_ref, pc_ref, pprev_ref, pnext_ref, x_ref, mod_ref, wo_ref, wp_ref, ps_ref, cw_ref,
                o_ref, *, seq_len):
    ts = x_ref.shape[1]
    i = pl.program_id(1)
    n_i = pl.num_programs(1)
    has_prev = (i > 0).astype(F32)
    has_next = (i < n_i - 1).astype(F32)

    pc = pc_ref[0]
    prev = pprev_ref[0] * has_prev
    nxt = pnext_ref[0] * has_next
    n_ext = ts + 2 * HALO
    u_ext = jnp.concatenate([prev[:, 0:256], pc[:, 0:256], nxt[:, 0:256]], axis=0)
    cu_ext = jnp.concatenate([prev[:, 512:768], pc[:, 512:768], nxt[:, 512:768]], axis=0)

    a2 = u_ext + pltpu.roll(u_ext, 1, 0)
    a4 = a2 + pltpu.roll(a2, 2, 0)
    a8 = a4 + pltpu.roll(a4, 4, 0)
    a16 = a8 + pltpu.roll(a8, 8, 0)
    lane = lax.broadcasted_iota(jnp.int32, (ts, POOL_WIDTH), 1)
    grp = lane // POOL_GROUP_DIM
    win = []
    for w, a in zip(POOL_WINDOWS, (a2, a4, a8, a16)):
        lead = w // 2 - 1
        r = a if lead == 0 else pltpu.roll(a, n_ext - lead, 0)
        win.append(r[HALO:HALO + ts])
    total = jnp.where(grp == 0, win[0], jnp.where(grp == 1, win[1], jnp.where(grp == 2, win[2], win[3])))
    half = jnp.where(grp == 0, 1, jnp.where(grp == 1, 2, jnp.where(grp == 2, 4, 8)))
    pos = i * ts + lax.broadcasted_iota(jnp.int32, (ts, POOL_WIDTH), 0)
    cnt = (jnp.minimum(pos + half, seq_len) - jnp.maximum(pos - half, 0)).astype(F32)
    diff = total / cnt - pc[:, 0:256]
    y_b = jnp.dot(diff.astype(BF16), wp_ref[...], preferred_element_type=F32) * ps_ref[...]

    cw = cw_ref[...]
    conv = (cw[0:1] * pltpu.roll(cu_ext, 1, 0) + cw[1:2] * cu_ext
            + cw[2:3] * pltpu.roll(cu_ext, n_ext - 1, 0))[HALO:HALO + ts]
    y_c = pc[:, 256:512] * conv

    yat = yat_ref[0, :, 0].reshape(ATTN_WIDTH, ts)
    mixed = lax.dot_general(yat, wo_ref[0:ATTN_WIDTH, :], (((0,), (0,)), ((), ())),
                            preferred_element_type=F32)
    mixed = mixed + jnp.dot(y_b.astype(BF16), wo_ref[ATTN_WIDTH:ATTN_WIDTH + POOL_WIDTH, :],
                            preferred_element_type=F32)
    mixed = mixed + jnp.dot(y_c.astype(BF16), wo_ref[ATTN_WIDTH + POOL_WIDTH:, :],
                            preferred_element_type=F32)
    o_ref[0] = x_ref[0] + mod_ref[0, 2:3, :] * mixed


def _mix(yat, pc, x, mod_l, w_out_bf, w_pool_bd, pool_scale_l, conv_w_l):
    b_, s_, d = x.shape
    ts = PROJ_TS
    assert ts == Q_TILE
    n_t = s_ // ts
    hb = ts // HALO
    n_hb = s_ // HALO
    return pl.pallas_call(
        functools.partial(_mix_kernel, seq_len=s_),
        grid=(b_, n_t),
        in_specs=[
            pl.BlockSpec((1, N_HEADS, 1, HEAD_DIM, ts), lambda b, i: (b, 0, i, 0, 0)),
            pl.BlockSpec((1, ts, 768), lambda b, i: (b, i, 0)),
            pl.BlockSpec((1, HALO, 768), lambda b, i: (b, jnp.maximum(i * hb - 1, 0), 0)),
            pl.BlockSpec((1, HALO, 768), lambda b, i: (b, jnp.minimum((i + 1) * hb, n_hb - 1), 0)),
            pl.BlockSpec((1, ts, d), lambda b, i: (b, i, 0)),
            pl.BlockSpec((1, 6, d), lambda b, i: (b, 0, 0)),
            pl.BlockSpec((d, d), lambda b, i: (0, 0)),
            pl.BlockSpec((POOL_WIDTH, POOL_WIDTH), lambda b, i: (0, 0)),
            pl.BlockSpec((1, POOL_WIDTH), lambda b, i: (0, 0)),
            pl.BlockSpec((3, CONV_WIDTH), lambda b, i: (0, 0)),
        ],
        out_specs=pl.BlockSpec((1, ts, d), lambda b, i: (b, i, 0)),
        out_shape=jax.ShapeDtypeStruct((b_, s_, d), F32),
        compiler_params=pltpu.CompilerParams(
            dimension_semantics=("parallel", "parallel"), vmem_limit_bytes=VMEM_LIMIT),
        name="mix_out_proj",
    )(yat, pc, pc, pc, x, mod_l, w_out_bf, w_pool_bd, pool_scale_l, conv_w_l)


def _ffn_kernel(x_ref, mod_ref, g_ref, wgu_ref, wd_ref, gf_ref, o_ref, *, final_norm):
    x = x_ref[0]
    ms = jnp.mean(x * x, axis=-1, keepdims=True)
    xn = x * lax.rsqrt(ms + EPS) * g_ref[...]
    h = (xn * (1.0 + mod_ref[0, 4:5, :]) + mod_ref[0, 3:4, :]).astype(BF16)
    gate = jnp.dot(h, wgu_ref[:, 0:D_FF], preferred_element_type=F32)
    up = jnp.dot(h, wgu_ref[:, D_FF:2 * D_FF], preferred_element_type=F32)
    act = (gate * jax.nn.sigmoid(gate) * up).astype(BF16)
    y = x + mod_ref[0, 5:6, :] * jnp.dot(act, wd_ref[...], preferred_element_type=F32)
    if final_norm:
        ms2 = jnp.mean(y * y, axis=-1, keepdims=True)
        y = y * lax.rsqrt(ms2 + EPS) * gf_ref[...]
    o_ref[0] = y


def _ffn(x, mod_l, g_ffn_l, w_gu_bf, w_down_bf, g_final, final_norm):
    b_, s_, d = x.shape
    ts = FFN_TS
    n_t = s_ // ts
    resident = pl.Buffered(1)
    return pl.pallas_call(
        functools.partial(_ffn_kernel, final_norm=final_norm),
        grid=(b_, n_t),
        in_specs=[
            pl.BlockSpec((1, ts, d), lambda b, i: (b, i, 0)),
            pl.BlockSpec((1, 6, d), lambda b, i: (b, 0, 0)),
            pl.BlockSpec((1, d), lambda b, i: (0, 0)),
            pl.BlockSpec((d, 2 * D_FF), lambda b, i: (0, 0), pipeline_mode=resident),
            pl.BlockSpec((D_FF, d), lambda b, i: (0, 0), pipeline_mode=resident),
            pl.BlockSpec((1, d), lambda b, i: (0, 0)),
        ],
        out_specs=pl.BlockSpec((1, ts, d), lambda b, i: (b, i, 0)),
        out_shape=jax.ShapeDtypeStruct((b_, s_, d), F32),
        compiler_params=pltpu.CompilerParams(
            dimension_semantics=("parallel", "parallel"), vmem_limit_bytes=VMEM_LIMIT),
        name="swiglu_ffn",
    )(x, mod_l, g_ffn_l, w_gu_bf, w_down_bf, g_final)


def _diagonal_bias():
    tq, tk = Q_TILE, K_TILE
    key_pos = np.arange(tq).reshape(tq // tk, tk)
    dist = np.abs(key_pos[:, :, None] - np.arange(tq)[None, None, :]).astype(np.float64)
    return jnp.asarray((-_slopes()[:, None, None, None] * LOG2E * dist[None]).astype(np.float32))


def kernel(x, c, w_ada, b_ada, g_mix, w_in, lambda_q1, lambda_k1, lambda_q2, lambda_k2, g_subln, w_pool,
           pool_scale, conv_w, w_out, g_ffn, w_gate_up, w_down, g_final):
    b_, s_, d = x.shape
    dtab = _diagonal_bias()
    mod = _modulation(c, w_ada, b_ada).reshape(DEPTH, b_, 6, d)
    g_final2 = g_final.reshape(1, d)
    for l in range(DEPTH):
        lambda_init = 0.8 - 0.6 * math.exp(-0.3 * l)
        lam_vecs = jnp.stack([lambda_q1[l], lambda_k1[l], lambda_q2[l], lambda_k2[l]]).astype(F32)
        g_sub = jnp.broadcast_to(
            (g_subln[l].astype(F32) * (1.0 - lambda_init)).reshape(N_HEADS, HEAD_DIM, 1),
            (N_HEADS, HEAD_DIM, Q_TILE))
        w_pool_bd = jax.scipy.linalg.block_diag(*[w_pool[l, g] for g in range(len(POOL_WINDOWS))]).astype(BF16)

        qt, kf, vt, pc = _projection(x, mod[l], g_mix[l].reshape(1, d), w_in[l].astype(BF16))
        yat = _attention(lam_vecs, qt, kf, vt, dtab, g_sub, lambda_init)
        x = _mix(yat, pc, x, mod[l], w_out[l].astype(BF16), w_pool_bd,
                 pool_scale[l].reshape(1, POOL_WIDTH), conv_w[l])
        x = _ffn(x, mod[l], g_ffn[l].reshape(1, d), w_gate_up[l].astype(BF16), w_down[l].astype(BF16),
                 g_final2, final_norm=(l == DEPTH - 1))
    return x
```

```python
import functools
import math

import numpy as np
import jax
import jax.numpy as jnp
from jax import lax
from jax.experimental import pallas as pl
from jax.experimental.pallas import tpu as pltpu

F32 = jnp.float32
BF16 = jnp.bfloat16

D_MODEL = 1024
DEPTH = 2
N_HEADS = 8
HEAD_DIM = 64
QK_DIM = 32
ATTN_WIDTH = N_HEADS * HEAD_DIM
POOL_WIDTH = 256
POOL_WINDOWS = (2, 4, 8, 16)
POOL_GROUP_DIM = 64
CONV_WIDTH = 256
IN_WIDTH = 3 * ATTN_WIDTH + POOL_WIDTH + 3 * CONV_WIDTH
D_FF = 2816
EPS = 1e-6
LOG2E = math.log2(math.e)
QK_SCALE = QK_DIM ** -0.5

LANES = 128
Q_TILE = 512
K_TILE = 256
SUBLANES = 8
HALO = 16
PROJ_TS = 512
FFN_TS = 512
VMEM_LIMIT = 56 * 1024 * 1024
NEG_BIG = -1e30
L_MAX = 2.0 ** 100
F32_HUGE = 3.0e38
LOOKAHEAD = 1
Q_PER_TRIP = 2
Q_SUB = 256

N_PIECES = 3
LANE_ROW = HEAD_DIM
LANE_COL = LANE_ROW + N_PIECES
LANE_END = LANE_COL + N_PIECES


def _slopes():
    return np.array([2.0 ** (-8.0 * (h + 1) / N_HEADS) for h in range(N_HEADS)], dtype=np.float64)


def _mod_kernel(c_ref, w_ref, b_ref, o_ref):
    c = c_ref[...]
    ca = c * jax.nn.sigmoid(c)
    o_ref[0] = jnp.dot(ca, w_ref[0], preferred_element_type=F32) + b_ref[0]


def _modulation(c, w_ada, b_ada):
    b_, d = c.shape
    n6 = w_ada.shape[-1] // d
    return pl.pallas_call(
        _mod_kernel,
        grid=(DEPTH, n6),
        in_specs=[
            pl.BlockSpec((b_, d), lambda l, n: (0, 0)),
            pl.BlockSpec((1, d, d), lambda l, n: (l, 0, n)),
            pl.BlockSpec((1, 1, d), lambda l, n: (l, 0, n)),
        ],
        out_specs=pl.BlockSpec((1, b_, d), lambda l, n: (l, 0, n)),
        out_shape=jax.ShapeDtypeStruct((DEPTH, b_, n6 * d), F32),
        compiler_params=pltpu.CompilerParams(
            dimension_semantics=("arbitrary", "arbitrary"), vmem_limit_bytes=VMEM_LIMIT),
        name="adaln_mod",
    )(c, w_ada, b_ada.reshape(DEPTH, 1, -1))


def _round_bf16(v):
    return v.astype(BF16).astype(F32)


def _pieces(v):
    out = []
    for _ in range(N_PIECES):
        p = _round_bf16(v)
        out.append(p)
        v = v - p
    return out


def _place(lane, first_lane, pieces, rest):
    for n, p in enumerate(pieces):
        rest = jnp.where(lane == first_lane + n, p, rest)
    return rest


def _proj_kernel(x_ref, mod_ref, g_ref, w_ref, qt_ref, kf_ref, vt_ref, pc_ref, *, seq_len):
    ts = x_ref.shape[1]
    x = x_ref[0]
    ms = jnp.mean(x * x, axis=-1, keepdims=True)
    xn = x * lax.rsqrt(ms + EPS) * g_ref[...]
    h = (xn * (1.0 + mod_ref[0, 1:2, :]) + mod_ref[0, 0:1, :]).astype(BF16)
    proj = jnp.dot(h, w_ref[...], preferred_element_type=F32)

    lane = lax.broadcasted_iota(jnp.int32, (ts, LANES), 1)
    row = lax.broadcasted_iota(jnp.int32, (ts, LANES), 0)
    pos = (pl.program_id(1) * ts + row - seq_len // 2).astype(F32) * LOG2E
    pos_pieces = _pieces(pos)
    zeros = jnp.zeros((ts, LANES), F32)
    q_pos = _place(lane, LANE_ROW, [-p for p in pos_pieces], zeros)
    k_pos = _place(lane, LANE_COL, pos_pieces, zeros)
    q_ones = jnp.where(jnp.logical_and(lane >= LANE_COL, lane < LANE_END), 1.0, 0.0)
    k_ones = jnp.where(jnp.logical_and(lane >= LANE_ROW, lane < LANE_COL), 1.0, 0.0)
    data_lane = lane < HEAD_DIM

    n_kt = ts // K_TILE
    for c in range(N_HEADS // 2):
        qc = proj[:, c * LANES:(c + 1) * LANES] * (QK_SCALE * LOG2E)
        kc = proj[:, ATTN_WIDTH + c * LANES:ATTN_WIDTH + (c + 1) * LANES]
        vc = proj[:, 2 * ATTN_WIDTH + c * LANES:2 * ATTN_WIDTH + (c + 1) * LANES]
        pairs = ((qc, kc), (pltpu.roll(qc, HEAD_DIM, 1), pltpu.roll(kc, HEAD_DIM, 1)))
        for e, (qq, kk) in enumerate(pairs):
            hd = 2 * c + e
            slope = float(_slopes()[hd])
            q_full = jnp.where(data_lane, qq, q_pos * slope + q_ones)
            qt_ref[0, hd, 0] = q_full.T.astype(BF16)
            kf_ref[0, hd] = jnp.where(data_lane, kk, k_pos * slope + k_ones).astype(BF16)
        vt = vc.T
        for e in range(2):
            hd = 2 * c + e
            for t in range(n_kt):
                vt_ref[0, hd, t] = vt[e * HEAD_DIM:(e + 1) * HEAD_DIM, t * K_TILE:(t + 1) * K_TILE].astype(BF16)

    o_p = 3 * ATTN_WIDTH
    pc_ref[0, :, 0:256] = proj[:, o_p:o_p + 256]
    pc_ref[0, :, 256:512] = proj[:, o_p + 256:o_p + 512]
    pc_ref[0, :, 512:768] = proj[:, o_p + 512:o_p + 768] * proj[:, o_p + 768:o_p + 1024]


def _projection(x, mod_l, g_mix_l, w_in_bf):
    b_, s_, d = x.shape
    ts = PROJ_TS
    assert ts == Q_TILE
    n_t = s_ // ts
    n_kt = ts // K_TILE
    return pl.pallas_call(
        functools.partial(_proj_kernel, seq_len=s_),
        grid=(b_, n_t),
        in_specs=[
            pl.BlockSpec((1, ts, d), lambda b, i: (b, i, 0)),
            pl.BlockSpec((1, 6, d), lambda b, i: (b, 0, 0)),
            pl.BlockSpec((1, d), lambda b, i: (0, 0)),
            pl.BlockSpec((d, IN_WIDTH), lambda b, i: (0, 0)),
        ],
        out_specs=[
            pl.BlockSpec((1, N_HEADS, 1, LANES, ts), lambda b, i: (b, 0, i, 0, 0)),
            pl.BlockSpec((1, N_HEADS, ts, LANES), lambda b, i: (b, 0, i, 0)),
            pl.BlockSpec((1, N_HEADS, n_kt, HEAD_DIM, K_TILE), lambda b, i: (b, 0, i, 0, 0)),
            pl.BlockSpec((1, ts, 768), lambda b, i: (b, i, 0)),
        ],
        out_shape=[
            jax.ShapeDtypeStruct((b_, N_HEADS, n_t, LANES, ts), BF16),
            jax.ShapeDtypeStruct((b_, N_HEADS, s_, LANES), BF16),
            jax.ShapeDtypeStruct((b_, N_HEADS, s_ // K_TILE, HEAD_DIM, K_TILE), BF16),
            jax.ShapeDtypeStruct((b_, s_, 768), F32),
        ],
        compiler_params=pltpu.CompilerParams(
            dimension_semantics=("parallel", "parallel"), vmem_limit_bytes=VMEM_LIMIT),
        name="in_proj",
    )(x, mod_l, g_mix_l, w_in_bf)


def _attn_kernel(lam_ref, qt_ref, kf_ref, vt_ref, dtab_ref, g_ref, o_ref,
                 flag_ref, acc_ref, den_ref, m_ref, *, lambda_init):
    tq, tk = Q_TILE, K_TILE
    n_q = qt_ref.shape[2]
    n_k = kf_ref.shape[2] // tk
    n_diag = tq // tk

    flag_ref[...] = jnp.zeros(flag_ref.shape, F32)

    lv = lam_ref[...]
    lam = (jnp.exp(jnp.sum(lv[0:1] * lv[1:2], axis=1, keepdims=True))
           - jnp.exp(jnp.sum(lv[2:3] * lv[3:4], axis=1, keepdims=True)) + lambda_init)

    def key_tile(qi, k):
        if isinstance(k, int) and k < n_diag:
            return n_diag * qi + k, None, k
        idx = k - n_diag
        after = jnp.where(idx >= n_diag * qi, 1, 0)
        return idx + n_diag * after, after, None

    def key_operand(qi, k, br):
        kj, after, table = key_tile(qi, k)
        kt = kf_ref[0, 0, pl.ds(pl.multiple_of(kj * tk, tk), tk), :]
        t_lane = lax.broadcasted_iota(jnp.int32, kt.shape, 1)
        pos_lane = jnp.logical_and(t_lane >= LANE_ROW, t_lane < LANE_END)
        if br == 0:
            drop = jnp.logical_and(t_lane >= QK_DIM, t_lane < HEAD_DIM)
        else:
            drop = t_lane < QK_DIM
        if table is not None:
            drop = jnp.logical_or(drop, pos_lane)
        else:
            sign = (1 - 2 * after).astype(F32)
            kt = kt * jnp.where(pos_lane, sign, 1.0).astype(BF16)
        return jnp.where(drop, jnp.zeros_like(kt), kt), kj, table

    def shifted_scores(qi, operand, cols=slice(None)):
        kt, kj, table = operand
        x = jnp.dot(kt, qt_ref[0, 0, qi, :, cols], preferred_element_type=F32)
        if table is not None:
            x = x + dtab_ref[0, table, :, cols]
        return x, kj

    def write_out(qi, a1, d1, a2, d2):
        o = a1 / d1 - lam * (a2 / d2)
        ms = jnp.mean(o * o, axis=0, keepdims=True)
        o_ref[0, 0, qi] = (o * lax.rsqrt(ms + EPS) * g_ref[0]).astype(BF16)

    def check_and_write(qi, acc, den):
        acc = [jnp.concatenate(a, axis=1) for a in acc]
        den = [jnp.sum(jnp.concatenate(d, axis=1), axis=0, keepdims=True) for d in den]
        bad = jnp.zeros((1, tq), F32)
        for a, d in zip(acc, den):
            mag = jnp.sum(jnp.abs(a), axis=0, keepdims=True)
            ok = jnp.logical_and(jnp.logical_and(d > 1.0 / L_MAX, d < L_MAX), mag < F32_HUGE)
            bad = jnp.maximum(bad, jnp.where(ok, 0.0, 1.0))
        flag_ref[...] = jnp.maximum(flag_ref[...], bad)
        write_out(qi, acc[0], den[0], acc[1], den[1])

    n_sub = tq // Q_SUB
    sub_cols = [slice(j * Q_SUB, (j + 1) * Q_SUB) for j in range(n_sub)]

    def fast_trip(i, carry):
        steps = [(Q_PER_TRIP * i + j, k) for j in range(Q_PER_TRIP) for k in range(n_k)]

        def scores(step):
            operands = [key_operand(step[0], step[1], br) for br in range(2)]
            return [[shifted_scores(step[0], op, cols) for op in operands] for cols in sub_cols]

        ahead = [scores(st) for st in steps[:LOOKAHEAD]]
        acc = [[None] * n_sub, [None] * n_sub]
        den = [[None] * n_sub, [None] * n_sub]
        for n, (qi, k) in enumerate(steps):
            cur = ahead.pop(0)
            nxt = []
            operands = None
            if n + LOOKAHEAD < len(steps):
                nq, nk = steps[n + LOOKAHEAD]
                operands = [key_operand(nq, nk, br) for br in range(2)]
            for j, cols in enumerate(sub_cols):
                if operands is not None:
                    nxt.append([shifted_scores(nq, op, cols) for op in operands])
                for br, (x, kj) in enumerate(cur[j]):
                    p = jnp.exp2(x)
                    pv = jnp.dot(vt_ref[0, 0, kj], p.astype(BF16), preferred_element_type=F32)
                    ps = jnp.sum(p.reshape(tk // SUBLANES, SUBLANES, Q_SUB), axis=0)
                    acc[br][j] = pv if k == 0 else acc[br][j] + pv
                    den[br][j] = ps if k == 0 else den[br][j] + ps
            if nxt:
                ahead.append(nxt)
            if k == n_k - 1:
                check_and_write(qi, acc, den)
        return carry

    lax.fori_loop(0, n_q // Q_PER_TRIP, fast_trip, 0)

    @pl.when(jnp.max(flag_ref[...]) > 0.0)
    def _safe_path():
        def tile_step(qi, k):
            for br in range(2):
                x, kj = shifted_scores(qi, key_operand(qi, k, br))
                m_old = m_ref[br]
                m_new = jnp.maximum(m_old, jnp.max(x, axis=0, keepdims=True))
                alpha = jnp.exp2(m_old - m_new)
                p = jnp.exp2(x - m_new)
                pv = jnp.dot(vt_ref[0, 0, kj], p.astype(BF16), preferred_element_type=F32)
                acc_ref[br] = alpha * acc_ref[br] + pv
                den_ref[br] = alpha * den_ref[br] + jnp.sum(p, axis=0, keepdims=True)
                m_ref[br] = m_new

        def safe_query_tile(qi, carry):
            m_ref[...] = jnp.full(m_ref.shape, NEG_BIG, F32)
            acc_ref[...] = jnp.zeros(acc_ref.shape, F32)
            den_ref[...] = jnp.zeros(den_ref.shape, F32)
            for k in range(n_diag):
                tile_step(qi, k)

            def off_diagonal(k, c):
                tile_step(qi, k)
                return c

            lax.fori_loop(n_diag, n_k, off_diagonal, 0)
            write_out(qi, acc_ref[0], den_ref[0], acc_ref[1], den_ref[1])
            return carry

        lax.fori_loop(0, n_q, safe_query_tile, 0)


def _attention(lam_vecs, qt, kf, vt, dtab, g_sub, lambda_init):
    b_, n_h, s_, _ = kf.shape
    tq, tk = Q_TILE, K_TILE
    n_q, n_k = s_ // tq, s_ // tk
    assert tq % tk == 0
    return pl.pallas_call(
        functools.partial(_attn_kernel, lambda_init=lambda_init),
        grid=(b_, n_h),
        in_specs=[
            pl.BlockSpec((4, QK_DIM), lambda b, h: (0, 0)),
            pl.BlockSpec((1, 1, n_q, LANES, tq), lambda b, h: (b, h, 0, 0, 0)),
            pl.BlockSpec((1, 1, s_, LANES), lambda b, h: (b, h, 0, 0)),
            pl.BlockSpec((1, 1, n_k, HEAD_DIM, tk), lambda b, h: (b, h, 0, 0, 0)),
            pl.BlockSpec((1, tq // tk, tk, tq), lambda b, h: (h, 0, 0, 0)),
            pl.BlockSpec((1, HEAD_DIM, tq), lambda b, h: (h, 0, 0)),
        ],
        out_specs=pl.BlockSpec((1, 1, n_q, HEAD_DIM, tq), lambda b, h: (b, h, 0, 0, 0)),
        out_shape=jax.ShapeDtypeStruct((b_, n_h, n_q, HEAD_DIM, tq), BF16),
        scratch_shapes=[
            pltpu.VMEM((1, tq), F32),
            pltpu.VMEM((2, HEAD_DIM, tq), F32),
            pltpu.VMEM((2, 1, tq), F32),
            pltpu.VMEM((2, 1, tq), F32),
        ],
        compiler_params=pltpu.CompilerParams(
            dimension_semantics=("parallel", "parallel"), vmem_limit_bytes=VMEM_LIMIT),
        name="diff_attn",
    )(lam_vecs, qt, kf, vt, dtab, g_sub)


def _mix_kernel(yat_ref, pc_ref, pprev_ref, pnext_ref, x_ref, mod_ref, wo_ref, wp_ref, ps_ref, cw_ref,
                o_ref, *, seq_len):
    ts = x_ref.shape[1]
    i = pl.program_id(1)
    n_i = pl.num_programs(1)
    has_prev = (i > 0).astype(F32)
    has_next = (i < n_i - 1).astype(F32)

    pc = pc_ref[0]
    prev = pprev_ref[0] * has_prev
    nxt = pnext_ref[0] * has_next
    n_ext = ts + 2 * HALO
    u_ext = jnp.concatenate([prev[:, 0:256], pc[:, 0:256], nxt[:, 0:256]], axis=0)
    cu_ext = jnp.concatenate([prev[:, 512:768], pc[:, 512:768], nxt[:, 512:768]], axis=0)

    a2 = u_ext + pltpu.roll(u_ext, 1, 0)
    a4 = a2 + pltpu.roll(a2, 2, 0)
    a8 = a4 + pltpu.roll(a4, 4, 0)
    a16 = a8 + pltpu.roll(a8, 8, 0)
    lane = lax.broadcasted_iota(jnp.int32, (ts, POOL_WIDTH), 1)
    grp = lane // POOL_GROUP_DIM
    win = []
    for w, a in zip(POOL_WINDOWS, (a2, a4, a8, a16)):
        lead = w // 2 - 1
        r = a if lead == 0 else pltpu.roll(a, n_ext - lead, 0)
        win.append(r[HALO:HALO + ts])
    total = jnp.where(grp == 0, win[0], jnp.where(grp == 1, win[1], jnp.where(grp == 2, win[2], win[3])))
    half = jnp.where(grp == 0, 1, jnp.where(grp == 1, 2, jnp.where(grp == 2, 4, 8)))
    pos = i * ts + lax.broadcasted_iota(jnp.int32, (ts, POOL_WIDTH), 0)
    cnt = (jnp.minimum(pos + half, seq_len) - jnp.maximum(pos - half, 0)).astype(F32)
    diff = total / cnt - pc[:, 0:256]
    y_b = jnp.dot(diff.astype(BF16), wp_ref[...], preferred_element_type=F32) * ps_ref[...]

    cw = cw_ref[...]
    conv = (cw[0:1] * pltpu.roll(cu_ext, 1, 0) + cw[1:2] * cu_ext
            + cw[2:3] * pltpu.roll(cu_ext, n_ext - 1, 0))[HALO:HALO + ts]
    y_c = pc[:, 256:512] * conv

    yat = yat_ref[0, :, 0].reshape(ATTN_WIDTH, ts)
    mixed = lax.dot_general(yat, wo_ref[0:ATTN_WIDTH, :], (((0,), (0,)), ((), ())),
                            preferred_element_type=F32)
    mixed = mixed + jnp.dot(y_b.astype(BF16), wo_ref[ATTN_WIDTH:ATTN_WIDTH + POOL_WIDTH, :],
                            preferred_element_type=F32)
    mixed = mixed + jnp.dot(y_c.astype(BF16), wo_ref[ATTN_WIDTH + POOL_WIDTH:, :],
                            preferred_element_type=F32)
    o_ref[0] = x_ref[0] + mod_ref[0, 2:3, :] * mixed


def _mix(yat, pc, x, mod_l, w_out_bf, w_pool_bd, pool_scale_l, conv_w_l):
    b_, s_, d = x.shape
    ts = PROJ_TS
    assert ts == Q_TILE
    n_t = s_ // ts
    hb = ts // HALO
    n_hb = s_ // HALO
    return pl.pallas_call(
        functools.partial(_mix_kernel, seq_len=s_),
        grid=(b_, n_t),
        in_specs=[
            pl.BlockSpec((1, N_HEADS, 1, HEAD_DIM, ts), lambda b, i: (b, 0, i, 0, 0)),
            pl.BlockSpec((1, ts, 768), lambda b, i: (b, i, 0)),
            pl.BlockSpec((1, HALO, 768), lambda b, i: (b, jnp.maximum(i * hb - 1, 0), 0)),
            pl.BlockSpec((1, HALO, 768), lambda b, i: (b, jnp.minimum((i + 1) * hb, n_hb - 1), 0)),
            pl.BlockSpec((1, ts, d), lambda b, i: (b, i, 0)),
            pl.BlockSpec((1, 6, d), lambda b, i: (b, 0, 0)),
            pl.BlockSpec((d, d), lambda b, i: (0, 0)),
            pl.BlockSpec((POOL_WIDTH, POOL_WIDTH), lambda b, i: (0, 0)),
            pl.BlockSpec((1, POOL_WIDTH), lambda b, i: (0, 0)),
            pl.BlockSpec((3, CONV_WIDTH), lambda b, i: (0, 0)),
        ],
        out_specs=pl.BlockSpec((1, ts, d), lambda b, i: (b, i, 0)),
        out_shape=jax.ShapeDtypeStruct((b_, s_, d), F32),
        compiler_params=pltpu.CompilerParams(
            dimension_semantics=("parallel", "parallel"), vmem_limit_bytes=VMEM_LIMIT),
        name="mix_out_proj",
    )(yat, pc, pc, pc, x, mod_l, w_out_bf, w_pool_bd, pool_scale_l, conv_w_l)


def _ffn_kernel(x_ref, mod_ref, g_ref, wgu_ref, wd_ref, gf_ref, o_ref, *, final_norm):
    x = x_ref[0]
    ms = jnp.mean(x * x, axis=-1, keepdims=True)
    xn = x * lax.rsqrt(ms + EPS) * g_ref[...]
    h = (xn * (1.0 + mod_ref[0, 4:5, :]) + mod_ref[0, 3:4, :]).astype(BF16)
    gate = jnp.dot(h, wgu_ref[:, 0:D_FF], preferred_element_type=F32)
    up = jnp.dot(h, wgu_ref[:, D_FF:2 * D_FF], preferred_element_type=F32)
    act = (gate * jax.nn.sigmoid(gate) * up).astype(BF16)
    y = x + mod_ref[0, 5:6, :] * jnp.dot(act, wd_ref[...], preferred_element_type=F32)
    if final_norm:
        ms2 = jnp.mean(y * y, axis=-1, keepdims=True)
        y = y * lax.rsqrt(ms2 + EPS) * gf_ref[...]
    o_ref[0] = y


def _ffn(x, mod_l, g_ffn_l, w_gu_bf, w_down_bf, g_final, final_norm):
    b_, s_, d = x.shape
    ts = FFN_TS
    n_t = s_ // ts
    resident = pl.Buffered(1)
    return pl.pallas_call(
        functools.partial(_ffn_kernel, final_norm=final_norm),
        grid=(b_, n_t),
        in_specs=[
            pl.BlockSpec((1, ts, d), lambda b, i: (b, i, 0)),
            pl.BlockSpec((1, 6, d), lambda b, i: (b, 0, 0)),
            pl.BlockSpec((1, d), lambda b, i: (0, 0)),
            pl.BlockSpec((d, 2 * D_FF), lambda b, i: (0, 0), pipeline_mode=resident),
            pl.BlockSpec((D_FF, d), lambda b, i: (0, 0), pipeline_mode=resident),
            pl.BlockSpec((1, d), lambda b, i: (0, 0)),
        ],
        out_specs=pl.BlockSpec((1, ts, d), lambda b, i: (b, i, 0)),
        out_shape=jax.ShapeDtypeStruct((b_, s_, d), F32),
        compiler_params=pltpu.CompilerParams(
            dimension_semantics=("parallel", "parallel"), vmem_limit_bytes=VMEM_LIMIT),
        name="swiglu_ffn",
    )(x, mod_l, g_ffn_l, w_gu_bf, w_down_bf, g_final)


def _diagonal_bias():
    tq, tk = Q_TILE, K_TILE
    key_pos = np.arange(tq).reshape(tq // tk, tk)
    dist = np.abs(key_pos[:, :, None] - np.arange(tq)[None, None, :]).astype(np.float64)
    return jnp.asarray((-_slopes()[:, None, None, None] * LOG2E * dist[None]).astype(np.float32))


def kernel(x, c, w_ada, b_ada, g_mix, w_in, lambda_q1, lambda_k1, lambda_q2, lambda_k2, g_subln, w_pool,
           pool_scale, conv_w, w_out, g_ffn, w_gate_up, w_down, g_final):
    b_, s_, d = x.shape
    dtab = _diagonal_bias()
    mod = _modulation(c, w_ada, b_ada).reshape(DEPTH, b_, 6, d)
    g_final2 = g_final.reshape(1, d)
    for l in range(DEPTH):
        lambda_init = 0.8 - 0.6 * math.exp(-0.3 * l)
        lam_vecs = jnp.stack([lambda_q1[l], lambda_k1[l], lambda_q2[l], lambda_k2[l]]).astype(F32)
        g_sub = jnp.broadcast_to(
            (g_subln[l].astype(F32) * (1.0 - lambda_init)).reshape(N_HEADS, HEAD_DIM, 1),
            (N_HEADS, HEAD_DIM, Q_TILE))
        w_pool_bd = jax.scipy.linalg.block_diag(*[w_pool[l, g] for g in range(len(POOL_WINDOWS))]).astype(BF16)

        qt, kf, vt, pc = _projection(x, mod[l], g_mix[l].reshape(1, d), w_in[l].astype(BF16))
        yat = _attention(lam_vecs, qt, kf, vt, dtab, g_sub, lambda_init)
        x = _mix(yat, pc, x, mod[l], w_out[l].astype(BF16), w_pool_bd,
                 pool_scale[l].reshape(1, POOL_WIDTH), conv_w[l])
        x = _ffn(x, mod[l], g_ffn[l].reshape(1, d), w_gate_up[l].astype(BF16), w_down[l].astype(BF16),
                 g_final2, final_norm=(l == DEPTH - 1))
    return x
```
